```python
import jax
import jax.numpy as jnp
from jax import lax
import numpy as np

D_MODEL = 1024
BATCH = 4
SEQ = 4096
DEPTH = 2
DEC_BATCH = 128
DEC_SEQ = 4
PAST_LEN = 16384
PAGE_SIZE = 128

N_MIXERS = 2
N_ATTN_LAYERS = (DEPTH + 1) // 2
N_CONV_LAYERS = DEPTH // 2
HEAD_DIM = 64
N_HEADS = D_MODEL // HEAD_DIM
N_KV_HEADS = 4
GQA_GROUP = N_HEADS // N_KV_HEADS
WINDOW = 128
BLOCK = WINDOW
WIN_BUF = min(WINDOW, PAST_LEN)
CONV_WIDTH = 31
D_CONV = D_MODEL
N_GROUPS = 4
EXPERTS_PER_GROUP = 8
N_EXPERTS = N_GROUPS * EXPERTS_PER_GROUP
TOP_K_IN_GROUP = 2
D_EXPERT = D_MODEL // 4
EPS = 1e-6
NEG_INF = -1e30

kernel_name = 'hybrid_swa_sink_conformer_hmoe_step'


def rms_norm(x, g):
    xf = x.astype(jnp.float32)
    y = xf * lax.rsqrt(jnp.mean(xf * xf, axis=-1, keepdims=True) + EPS)
    return (y * g.astype(jnp.float32)).astype(x.dtype)


def layer_norm(x, g, b):
    xf = x.astype(jnp.float32)
    xc = xf - jnp.mean(xf, axis=-1, keepdims=True)
    var = jnp.mean(xc * xc, axis=-1, keepdims=True)
    y = xc * lax.rsqrt(var + EPS) * g.astype(jnp.float32) + b.astype(jnp.float32)
    return y.astype(x.dtype)


def alibi_slopes():
    h = jnp.arange(1, N_HEADS + 1, dtype=jnp.float32)
    return jnp.exp2(-8.0 * h / N_HEADS).reshape(N_KV_HEADS, GQA_GROUP)


def sink_softmax(logits, sink):
    m = jnp.maximum(jnp.max(logits, axis=-1, keepdims=True), sink)
    p = jnp.exp(logits - m)
    return p / (jnp.sum(p, axis=-1, keepdims=True) + jnp.exp(sink - m))


def attend_probs(scores, dist, valid, sinks):
    slopes = alibi_slopes()[:, :, None, None]
    logits = scores * (HEAD_DIM ** -0.5) - slopes * dist.astype(jnp.float32)
    logits = jnp.where(valid, logits, NEG_INF)
    sink = sinks.astype(jnp.float32).reshape(N_KV_HEADS, GQA_GROUP)[:, :, None, None]
    return sink_softmax(logits, sink)


def split_qkv(h, w_qkv):
    b, t, _ = h.shape
    qkv = jnp.einsum('btd,de->bte', h, w_qkv)
    nq = N_HEADS * HEAD_DIM
    nk = N_KV_HEADS * HEAD_DIM
    q = qkv[..., :nq].reshape(b, t, N_KV_HEADS, GQA_GROUP, HEAD_DIM)
    k = qkv[..., nq:nq + nk].reshape(b, t, N_KV_HEADS, HEAD_DIM)
    v = qkv[..., nq + nk:].reshape(b, t, N_KV_HEADS, HEAD_DIM)
    return q, k, v


def attn_prompt(h, w_qkv, w_o, sinks):
    b, t, _ = h.shape
    nb = t // BLOCK
    q, k, v = split_qkv(h, w_qkv)
    qb = q.reshape(b, nb, BLOCK, N_KV_HEADS, GQA_GROUP, HEAD_DIM)

    def band(xf):
        xb = xf.reshape(b, nb, BLOCK, N_KV_HEADS, HEAD_DIM)
        prev = jnp.concatenate([jnp.zeros_like(xb[:, :1]), xb[:, :-1]], axis=1)
        return jnp.concatenate([prev, xb], axis=2)

    k_band, v_band = band(k), band(v)
    scores = jnp.einsum('bnqkgd,bnskd->bnkgqs', qb, k_band,
                        preferred_element_type=jnp.float32)
    qi = jnp.arange(BLOCK)[:, None]
    kj = jnp.arange(2 * BLOCK)[None, :]
    dist = qi + BLOCK - kj
    key_pos = jnp.arange(nb)[:, None, None] * BLOCK + kj[None] - BLOCK
    valid = (dist >= 0) & (dist < WINDOW) & (key_pos >= 0)
    p = attend_probs(scores, dist, valid[None, :, None, None], sinks)
    o = jnp.einsum('bnkgqs,bnskd->bnqkgd', p.astype(v_band.dtype), v_band)
    o = o.reshape(b, t, N_HEADS * HEAD_DIM)
    y = jnp.einsum('bte,ed->btd', o, w_o)
    return y, k[:, t - WIN_BUF:], v[:, t - WIN_BUF:]


def attn_sample(h, cache_k, cache_v, w_qkv, w_o, sinks):
    b, t, _ = h.shape
    n_past = cache_k.shape[1]
    q, k, v = split_qkv(h, w_qkv)
    k_all = jnp.concatenate([cache_k.astype(k.dtype), k], axis=1)
    v_all = jnp.concatenate([cache_v.astype(v.dtype), v], axis=1)
    scores = jnp.einsum('bqkgd,bskd->bkgqs', q, k_all,
                        preferred_element_type=jnp.float32)
    dist = (n_past + jnp.arange(t))[:, None] - jnp.arange(n_past + t)[None, :]
    valid = (dist >= 0) & (dist < WINDOW)
    p = attend_probs(scores, dist, valid, sinks)
    o = jnp.einsum('bkgqs,bskd->bqkgd', p.astype(v_all.dtype), v_all)
    o = o.reshape(b, t, N_HEADS * HEAD_DIM)
    y = jnp.einsum('bte,ed->btd', o, w_o)
    return y, k_all[:, -WIN_BUF:], v_all[:, -WIN_BUF:]


def conv_module(h, hist, w_pw1, b_pw1, w_dw, b_dw, ln_g, ln_b, w_pw2, b_pw2):
    a = jnp.einsum('btd,de->bte', h, w_pw1) + b_pw1
    u = a[..., :D_CONV] * jax.nn.sigmoid(a[..., D_CONV:])
    ext = jnp.concatenate([hist.astype(u.dtype), u], axis=1)
    c = lax.conv_general_dilated(ext, w_dw[:, None, :].astype(ext.dtype),
                                 window_strides=(1,), padding='VALID',
                                 dimension_numbers=('NWC', 'WIO', 'NWC'),
                                 feature_group_count=D_CONV) + b_dw
    c = jax.nn.silu(layer_norm(c, ln_g, ln_b))
    y = jnp.einsum('btc,cd->btd', c, w_pw2) + b_pw2
    return y, ext[:, -(CONV_WIDTH - 1):]


def hier_moe(h, w_gr, b_gr, w_er, b_er, w_gate, w_up, w_down):
    n_tok = h.shape[0]
    g_logits = jnp.einsum('td,dg->tg', h, w_gr).astype(jnp.float32) + b_gr.astype(jnp.float32)
    g_prob = jax.nn.softmax(g_logits, axis=-1)
    g_w, g_idx = lax.top_k(g_prob, 1)
    e_logits = (jnp.einsum('td,de->te', h, w_er).astype(jnp.float32)
                + b_er.astype(jnp.float32)).reshape(n_tok, N_GROUPS, EXPERTS_PER_GROUP)
    e_sel = jnp.take_along_axis(e_logits, g_idx[:, :, None], axis=1)[:, 0]
    e_prob = jax.nn.softmax(e_sel, axis=-1)
    top_v, top_i = lax.top_k(e_prob, TOP_K_IN_GROUP)
    top_v = top_v / jnp.sum(top_v, axis=-1, keepdims=True)
    expert_id = g_idx * EXPERTS_PER_GROUP + top_i
    weights = top_v * g_w
    combine = jnp.sum(jax.nn.one_hot(expert_id, N_EXPERTS, dtype=jnp.float32)
                      * weights[..., None], axis=1)
    out = jnp.zeros(h.shape, jnp.float32)
    for e in range(N_EXPERTS):
        hid = jax.nn.silu(h @ w_gate[e]) * (h @ w_up[e])
        out = out + combine[:, e:e + 1] * (hid @ w_down[e]).astype(jnp.float32)
    return out.astype(h.dtype)


def setup_inputs(seed: int = 0) -> dict:
    key = jax.random.key(seed)
    ks = jax.random.split(key, 32)
    f32 = jnp.float32

    def nrm(k, shape, scale):
        return jax.random.normal(k, shape, f32) * scale

    qkv_out = (N_HEADS + 2 * N_KV_HEADS) * HEAD_DIM
    return {
        'x_prompt': nrm(ks[0], (BATCH, SEQ, D_MODEL), 1.0),
        'x_sample': nrm(ks[1], (DEC_BATCH, DEC_SEQ, D_MODEL), 1.0),
        'cache_k': nrm(ks[2], (N_ATTN_LAYERS, DEC_BATCH, WIN_BUF, N_KV_HEADS, HEAD_DIM), 1.0),
        'cache_v': nrm(ks[3], (N_ATTN_LAYERS, DEC_BATCH, WIN_BUF, N_KV_HEADS, HEAD_DIM), 1.0),
        'state_conv': nrm(ks[4], (N_CONV_LAYERS, DEC_BATCH, CONV_WIDTH - 1, D_CONV), 1.0),
        'w_qkv': nrm(ks[5], (N_ATTN_LAYERS, D_MODEL, qkv_out), D_MODEL ** -0.5),
        'w_o': nrm(ks[6], (N_ATTN_LAYERS, N_HEADS * HEAD_DIM, D_MODEL), (N_HEADS * HEAD_DIM) ** -0.5),
        'attn_sinks': nrm(ks[7], (N_ATTN_LAYERS, N_HEADS), 0.5),
        'w_pw1': nrm(ks[8], (N_CONV_LAYERS, D_MODEL, 2 * D_CONV), D_MODEL ** -0.5),
        'b_pw1': nrm(ks[9], (N_CONV_LAYERS, 2 * D_CONV), 0.02),
        'w_dw': nrm(ks[10], (N_CONV_LAYERS, CONV_WIDTH, D_CONV), CONV_WIDTH ** -0.5),
        'b_dw': nrm(ks[11], (N_CONV_LAYERS, D_CONV), 0.02),
        'conv_ln_g': 1.0 + nrm(ks[12], (N_CONV_LAYERS, D_CONV), 0.02),
        'conv_ln_b': nrm(ks[13], (N_CONV_LAYERS, D_CONV), 0.02),
        'w_pw2': nrm(ks[14], (N_CONV_LAYERS, D_CONV, D_MODEL), D_CONV ** -0.5),
        'b_pw2': nrm(ks[15], (N_CONV_LAYERS, D_MODEL), 0.02),
        'norm_mix_g': 1.0 + nrm(ks[16], (DEPTH, D_MODEL), 0.02),
        'norm_ffn_g': 1.0 + nrm(ks[17], (DEPTH, D_MODEL), 0.02),
        'w_group_router': nrm(ks[18], (DEPTH, D_MODEL, N_GROUPS), D_MODEL ** -0.5),
        'b_group_router': nrm(ks[19], (DEPTH, N_GROUPS), 0.01),
        'w_expert_router': nrm(ks[20], (DEPTH, D_MODEL, N_EXPERTS), D_MODEL ** -0.5),
        'b_expert_router': nrm(ks[21], (DEPTH, N_EXPERTS), 0.01),
        'w_gate': nrm(ks[22], (DEPTH, N_EXPERTS, D_MODEL, D_EXPERT), D_MODEL ** -0.5),
        'w_up': nrm(ks[23], (DEPTH, N_EXPERTS, D_MODEL, D_EXPERT), D_MODEL ** -0.5),
        'w_down': nrm(ks[24], (DEPTH, N_EXPERTS, D_EXPERT, D_MODEL), D_EXPERT ** -0.5),
        'final_norm_g': 1.0 + nrm(ks[25], (D_MODEL,), 0.02),
    }


def reference(x_prompt, x_sample, cache_k, cache_v, state_conv,
              w_qkv, w_o, attn_sinks,
              w_pw1, b_pw1, w_dw, b_dw, conv_ln_g, conv_ln_b, w_pw2, b_pw2,
              norm_mix_g, norm_ffn_g,
              w_group_router, b_group_router, w_expert_router, b_expert_router,
              w_gate, w_up, w_down, final_norm_g):
    xp, xs = x_prompt, x_sample
    n_prompt_tok = xp.shape[0] * xp.shape[1]
    k_p, v_p, c_p, k_s, v_s, c_s = [], [], [], [], [], []
    for i in range(DEPTH):
        hp = rms_norm(xp, norm_mix_g[i])
        hs = rms_norm(xs, norm_mix_g[i])
        j = i // N_MIXERS
        if i % N_MIXERS == 0:
            yp, kp, vp = attn_prompt(hp, w_qkv[j], w_o[j], attn_sinks[j])
            ys, kn, vn = attn_sample(hs, cache_k[j], cache_v[j], w_qkv[j], w_o[j], attn_sinks[j])
            k_p.append(kp)
            v_p.append(vp)
            k_s.append(kn)
            v_s.append(vn)
        else:
            conv_params = (w_pw1[j], b_pw1[j], w_dw[j], b_dw[j], conv_ln_g[j], conv_ln_b[j],
                           w_pw2[j], b_pw2[j])
            hist0 = jnp.zeros((xp.shape[0], CONV_WIDTH - 1, D_CONV), xp.dtype)
            yp, cp = conv_module(hp, hist0, *conv_params)
            ys, cn = conv_module(hs, state_conv[j], *conv_params)
            c_p.append(cp)
            c_s.append(cn)
        xp = xp + yp
        xs = xs + ys
        h_all = jnp.concatenate([rms_norm(xp, norm_ffn_g[i]).reshape(-1, D_MODEL),
                                 rms_norm(xs, norm_ffn_g[i]).reshape(-1, D_MODEL)], axis=0)
        m = hier_moe(h_all, w_group_router[i], b_group_router[i], w_expert_router[i],
                     b_expert_router[i], w_gate[i], w_up[i], w_down[i])
        xp = xp + m[:n_prompt_tok].reshape(xp.shape)
        xs = xs + m[n_prompt_tok:].reshape(xs.shape)
    y_prompt = rms_norm(xp, final_norm_g)
    y_sample = rms_norm(xs, final_norm_g)
    return (y_prompt, y_sample, jnp.stack(k_p), jnp.stack(v_p), jnp.stack(c_p),
            jnp.stack(k_s), jnp.stack(v_s), jnp.stack(c_s))
```

```python
import functools

import jax
import jax.numpy as jnp
from jax import lax
from jax.experimental import pallas as pl
from jax.experimental.pallas import tpu as pltpu

D_MODEL = 1024
BATCH = 4
SEQ = 4096
DEPTH = 2
DEC_BATCH = 128
DEC_SEQ = 4
HEAD_DIM = 64
N_HEADS = 16
N_KV_HEADS = 4
GQA_GROUP = 4
WINDOW = 128
BLOCK = 128
WIN_BUF = 128
CONV_WIDTH = 31
HIST = CONV_WIDTH - 1
N_GROUPS = 4
EXPERTS_PER_GROUP = 8
N_EXPERTS = 32
D_EXPERT = 256
EPS = 1e-6
NEG_INF = -1e30

T_PROMPT = BATCH * SEQ
T_SAMPLE = DEC_BATCH * DEC_SEQ
T_ALL = T_PROMPT + T_SAMPLE
KV_DIM = N_KV_HEADS * HEAD_DIM
QKV_OUT = D_MODEL + 2 * KV_DIM

LANES = 128
TOKEN_TILE = 512
MOE_TOKEN_TILE = 1536
CONV_TILE = 512
CONV_HALO = 32
SAMPLE_BATCH_BLOCK = 8
VMEM_LIMIT = 48 * 1024 * 1024

F32 = jnp.float32
BF16 = jnp.bfloat16


def _params(*sem):
    return pltpu.CompilerParams(dimension_semantics=sem, vmem_limit_bytes=VMEM_LIMIT)


def _rms(x, g):
    return x * lax.rsqrt(jnp.mean(x * x, axis=-1, keepdims=True) + EPS) * g


def _mm(a, b, precise, contract_b=0):
    dims = (((1,), (contract_b,)), ((), ()))
    if precise:
        return lax.dot_general(a.astype(F32), b.astype(F32), dims, precision=lax.Precision.HIGHEST,
                               preferred_element_type=F32)
    return lax.dot_general(a.astype(BF16), b.astype(BF16), dims, preferred_element_type=F32)


def _qkv_kernel(x_ref, g_ref, w_ref, q_ref, kv_ref, *, precise):
    qkv = _mm(_rms(x_ref[...], g_ref[...]), w_ref[...], precise)
    q_ref[...] = (qkv[:, :D_MODEL] * (HEAD_DIM ** -0.5)).astype(q_ref.dtype)
    kv_ref[...] = qkv[:, D_MODEL:]


def _qkv_precise_kernel(tbl_ref, x_ref, g_ref, w_ref, q_ref, kv_ref):
    del tbl_ref
    _qkv_kernel(x_ref, g_ref, w_ref, q_ref, kv_ref, precise=True)


def _qkv_precise(blocks, x, g, w):
    n = blocks.shape[0]
    return pl.pallas_call(
        _qkv_precise_kernel,
        grid_spec=pltpu.PrefetchScalarGridSpec(
            num_scalar_prefetch=1,
            grid=(n,),
            in_specs=[pl.BlockSpec((BLOCK, D_MODEL), lambda i, t: (t[i], 0)),
                      pl.BlockSpec((1, D_MODEL), lambda i, t: (0, 0)),
                      pl.BlockSpec((D_MODEL, QKV_OUT), lambda i, t: (0, 0))],
            out_specs=[pl.BlockSpec((BLOCK, D_MODEL), lambda i, t: (i, 0)),
                       pl.BlockSpec((BLOCK, 2 * KV_DIM), lambda i, t: (i, 0))],
        ),
        out_shape=[jax.ShapeDtypeStruct((n * BLOCK, D_MODEL), F32),
                   jax.ShapeDtypeStruct((n * BLOCK, 2 * KV_DIM), F32)],
        compiler_params=_params("arbitrary"),
        name="qkv_proj_precise",
    )(blocks, x, g, w)


def _qkv(x, g, w):
    n = x.shape[0] // TOKEN_TILE
    return pl.pallas_call(
        functools.partial(_qkv_kernel, precise=False),
        grid=(n,),
        in_specs=[pl.BlockSpec((TOKEN_TILE, D_MODEL), lambda i: (i, 0)),
                  pl.BlockSpec((1, D_MODEL), lambda i: (0, 0)),
                  pl.BlockSpec((D_MODEL, QKV_OUT), lambda i: (0, 0))],
        out_specs=[pl.BlockSpec((TOKEN_TILE, D_MODEL), lambda i: (i, 0)),
                   pl.BlockSpec((TOKEN_TILE, 2 * KV_DIM), lambda i: (i, 0))],
        out_shape=[jax.ShapeDtypeStruct((x.shape[0], D_MODEL), BF16),
                   jax.ShapeDtypeStruct((x.shape[0], 2 * KV_DIM), F32)],
        compiler_params=_params("parallel"),
        name="qkv_proj",
    )(x, g, w)


def _alibi_slope(head):
    return 2.0 ** (-8.0 * (head + 1) / N_HEADS)


def _softmax_pv(s, dist_f, valid, slope, sink, v, precise):
    logits = jnp.where(valid, s - slope * dist_f, NEG_INF)
    m = jnp.maximum(jnp.max(logits, axis=-1, keepdims=True), sink)
    p = jnp.exp(logits - m)
    denom = jnp.sum(p, axis=-1, keepdims=True) + jnp.exp(sink - m)
    return _mm(p, v, precise) / denom


def _attn_prompt_kernel(sink_ref, q_ref, kvp_ref, kvc_ref, o_ref, *, precise, first_block):
    q = q_ref[...]
    kv = jnp.concatenate([kvp_ref[...], kvc_ref[...]], axis=0)
    kv = kv if precise else kv.astype(BF16)
    qi = lax.broadcasted_iota(jnp.int32, (BLOCK, 2 * BLOCK), 0)
    kj = lax.broadcasted_iota(jnp.int32, (BLOCK, 2 * BLOCK), 1)
    dist = qi + BLOCK - kj
    valid = (dist >= 0) & (dist < WINDOW) & ((kj >= BLOCK) | jnp.logical_not(first_block()))
    dist_f = dist.astype(F32)
    for kvh in range(N_KV_HEADS):
        k_h = kv[:, kvh * HEAD_DIM:(kvh + 1) * HEAD_DIM]
        v_h = kv[:, KV_DIM + kvh * HEAD_DIM:KV_DIM + (kvh + 1) * HEAD_DIM]
        for g in range(GQA_GROUP):
            head = kvh * GQA_GROUP + g
            q_h = q[:, head * HEAD_DIM:(head + 1) * HEAD_DIM]
            s = _mm(q_h, k_h, precise, contract_b=1)
            o = _softmax_pv(s, dist_f, valid, _alibi_slope(head), sink_ref[head], v_h, precise)
            o_ref[:, head * HEAD_DIM:(head + 1) * HEAD_DIM] = o.astype(o_ref.dtype)


def _attn_prompt_last_precise(sinks, q_c, kv_c):
    return pl.pallas_call(
        functools.partial(_attn_prompt_kernel, precise=True, first_block=lambda: SEQ // BLOCK == 1),
        grid_spec=pltpu.PrefetchScalarGridSpec(
            num_scalar_prefetch=1,
            grid=(BATCH,),
            in_specs=[pl.BlockSpec((BLOCK, D_MODEL), lambda b, s: (2 * b + 1, 0)),
                      pl.BlockSpec((BLOCK, 2 * KV_DIM), lambda b, s: (2 * b, 0)),
                      pl.BlockSpec((BLOCK, 2 * KV_DIM), lambda b, s: (2 * b + 1, 0))],
            out_specs=pl.BlockSpec((BLOCK, D_MODEL), lambda b, s: (b, 0)),
        ),
        out_shape=jax.ShapeDtypeStruct((BATCH * BLOCK, D_MODEL), F32),
        compiler_params=_params("arbitrary"),
        name="attn_prompt_precise",
    )(sinks, q_c, kv_c, kv_c)


def _attn_prompt(sinks, q_all, kv_all):
    nb = SEQ // BLOCK
    return pl.pallas_call(
        functools.partial(_attn_prompt_kernel, precise=False, first_block=lambda: pl.program_id(1) == 0),
        grid_spec=pltpu.PrefetchScalarGridSpec(
            num_scalar_prefetch=1,
            grid=(BATCH, nb),
            in_specs=[pl.BlockSpec((BLOCK, D_MODEL), lambda b, i, s: (b * nb + i, 0)),
                      pl.BlockSpec((BLOCK, 2 * KV_DIM), lambda b, i, s: (b * nb + jnp.maximum(i - 1, 0), 0)),
                      pl.BlockSpec((BLOCK, 2 * KV_DIM), lambda b, i, s: (b * nb + i, 0))],
            out_specs=pl.BlockSpec((BLOCK, D_MODEL), lambda b, i, s: (b * nb + i, 0)),
        ),
        out_shape=jax.ShapeDtypeStruct((T_PROMPT, D_MODEL), BF16),
        compiler_params=_params("parallel", "parallel"),
        name="attn_prompt",
    )(sinks, q_all, kv_all, kv_all)


SAMPLE_KEYS = WIN_BUF + 16


def _attn_sample_kernel(sink_ref, q_ref, kvn_ref, ck_ref, cv_ref, o_ref, *, precise):
    rows = GQA_GROUP * DEC_SEQ
    t_q = lax.broadcasted_iota(jnp.int32, (rows, SAMPLE_KEYS), 0) % DEC_SEQ
    s_k = lax.broadcasted_iota(jnp.int32, (rows, SAMPLE_KEYS), 1)
    dist = WIN_BUF + t_q - s_k
    valid = (dist >= 0) & (dist < WINDOW) & (s_k < WIN_BUF + DEC_SEQ)
    dist_f = dist.astype(F32)
    g_row = lax.broadcasted_iota(jnp.int32, (rows, 1), 0) // DEC_SEQ
    pad = jnp.zeros((SAMPLE_KEYS - WIN_BUF - DEC_SEQ, KV_DIM), F32)
    for b in range(SAMPLE_BATCH_BLOCK):
        r0 = b * DEC_SEQ
        q_b = q_ref[r0:r0 + DEC_SEQ, :]
        kvn = kvn_ref[r0:r0 + DEC_SEQ, :]
        k_all = jnp.concatenate([ck_ref[b], kvn[:, :KV_DIM], pad], axis=0)
        v_all = jnp.concatenate([cv_ref[b], kvn[:, KV_DIM:], pad], axis=0)
        if not precise:
            k_all, v_all = k_all.astype(BF16), v_all.astype(BF16)
        for kvh in range(N_KV_HEADS):
            k_h = k_all[:, kvh * HEAD_DIM:(kvh + 1) * HEAD_DIM]
            v_h = v_all[:, kvh * HEAD_DIM:(kvh + 1) * HEAD_DIM]
            q_h = jnp.concatenate(
                [q_b[:, (kvh * GQA_GROUP + g) * HEAD_DIM:(kvh * GQA_GROUP + g + 1) * HEAD_DIM]
                 for g in range(GQA_GROUP)], axis=0)
            s = _mm(q_h, k_h, precise, contract_b=1)
            slope = jnp.zeros((rows, 1), F32)
            sink = jnp.zeros((rows, 1), F32)
            for g in range(GQA_GROUP):
                head = kvh * GQA_GROUP + g
                slope = jnp.where(g_row == g, _alibi_slope(head), slope)
                sink = jnp.where(g_row == g, sink_ref[head], sink)
            o = _softmax_pv(s, dist_f, valid, slope, sink, v_h, precise).astype(o_ref.dtype)
            for g in range(GQA_GROUP):
                head = kvh * GQA_GROUP + g
                o_ref[r0:r0 + DEC_SEQ, head * HEAD_DIM:(head + 1) * HEAD_DIM] = (
                    o[g * DEC_SEQ:(g + 1) * DEC_SEQ, :])


def _attn_sample(sinks, q_all, kv_all, cache_k, cache_v, first_row, precise):
    bb = SAMPLE_BATCH_BLOCK
    rows = bb * DEC_SEQ
    first = first_row // rows
    return pl.pallas_call(
        functools.partial(_attn_sample_kernel, precise=precise),
        grid_spec=pltpu.PrefetchScalarGridSpec(
            num_scalar_prefetch=1,
            grid=(DEC_BATCH // bb,),
            in_specs=[pl.BlockSpec((rows, D_MODEL), lambda i, s: (first + i, 0)),
                      pl.BlockSpec((rows, 2 * KV_DIM), lambda i, s: (first + i, 0)),
                      pl.BlockSpec((bb, WIN_BUF, KV_DIM), lambda i, s: (i, 0, 0)),
                      pl.BlockSpec((bb, WIN_BUF, KV_DIM), lambda i, s: (i, 0, 0))],
            out_specs=pl.BlockSpec((rows, D_MODEL), lambda i, s: (i, 0)),
        ),
        out_shape=jax.ShapeDtypeStruct((T_SAMPLE, D_MODEL), F32 if precise else BF16),
        compiler_params=_params("parallel"),
        name="attn_sample_precise" if precise else "attn_sample",
    )(sinks, q_all, kv_all, cache_k, cache_v)


ROUTER_COLS = LANES


def _route(logits):
    lane = lax.broadcasted_iota(jnp.int32, logits.shape, 1)
    is_group = (lane >= N_EXPERTS) & (lane < N_EXPERTS + N_GROUPS)
    gl = jnp.where(is_group, logits, -jnp.inf)
    g_max = jnp.max(gl, axis=-1, keepdims=True)
    g_idx = jnp.min(jnp.where(gl == g_max, lane - N_EXPERTS, N_GROUPS), axis=-1, keepdims=True)
    g_w = 1.0 / jnp.sum(jnp.exp(gl - g_max), axis=-1, keepdims=True)
    in_group = (lane < N_EXPERTS) & ((lane // EXPERTS_PER_GROUP) == g_idx)
    el = jnp.where(in_group, logits, -jnp.inf)
    m1 = jnp.max(el, axis=-1, keepdims=True)
    i1 = jnp.min(jnp.where(el == m1, lane, ROUTER_COLS), axis=-1, keepdims=True)
    el2 = jnp.where(lane == i1, -jnp.inf, el)
    m2 = jnp.max(el2, axis=-1, keepdims=True)
    i2 = jnp.min(jnp.where(el2 == m2, lane, ROUTER_COLS), axis=-1, keepdims=True)
    r = jnp.exp(m2 - m1)
    w1 = g_w / (1.0 + r)
    w2 = g_w * r / (1.0 + r)
    return jnp.where(lane == i1, w1, jnp.where(lane == i2, w2, 0.0))


def _post_kernel(a_ref, w_ref, b_ref, x_ref, g_ref, wr_ref, br_ref, x1_ref, h_ref, c_ref, *, precise):
    x1 = x_ref[...] + (_mm(a_ref[...], w_ref[...], precise) + b_ref[...])
    x1_ref[...] = x1
    h = _rms(x1, g_ref[...])
    h_ref[...] = h.astype(BF16)
    c_ref[...] = _route(_mm(h, wr_ref[...], precise) + br_ref[...])


def _post_precise_kernel(tbl_ref, a_ref, w_ref, b_ref, x_ref, g_ref, wr_ref, br_ref, x1_in, h_in, c_in,
                         x1_ref, h_ref, c_ref):
    del tbl_ref, x1_in, h_in, c_in
    _post_kernel(a_ref, w_ref, b_ref, x_ref, g_ref, wr_ref, br_ref, x1_ref, h_ref, c_ref, precise=True)


def _post_precise(blocks, a_c, w, b, x, g, wr, br, x1, h, c):
    n = blocks.shape[0]
    listed = lambda i, t: (t[i], 0)
    const = lambda i, t: (0, 0)
    return pl.pallas_call(
        _post_precise_kernel,
        grid_spec=pltpu.PrefetchScalarGridSpec(
            num_scalar_prefetch=1,
            grid=(n,),
            in_specs=[pl.BlockSpec((BLOCK, D_MODEL), lambda i, t: (i, 0)),
                      pl.BlockSpec((D_MODEL, D_MODEL), const),
                      pl.BlockSpec((1, D_MODEL), const),
                      pl.BlockSpec((BLOCK, D_MODEL), listed),
                      pl.BlockSpec((1, D_MODEL), const),
                      pl.BlockSpec((D_MODEL, ROUTER_COLS), const),
                      pl.BlockSpec((1, ROUTER_COLS), const),
                      pl.BlockSpec(memory_space=pl.ANY),
                      pl.BlockSpec(memory_space=pl.ANY),
                      pl.BlockSpec(memory_space=pl.ANY)],
            out_specs=[pl.BlockSpec((BLOCK, D_MODEL), listed),
                       pl.BlockSpec((BLOCK, D_MODEL), listed),
                       pl.BlockSpec((BLOCK, ROUTER_COLS), listed)],
        ),
        out_shape=[jax.ShapeDtypeStruct(x1.shape, x1.dtype),
                   jax.ShapeDtypeStruct(h.shape, h.dtype),
                   jax.ShapeDtypeStruct(c.shape, c.dtype)],
        input_output_aliases={8: 0, 9: 1, 10: 2},
        compiler_params=_params("arbitrary"),
        name="mixer_out_route_precise",
    )(blocks, a_c, w, b, x, g, wr, br, x1, h, c)


def _post(a, w, b, x, g, wr, br):
    n = x.shape[0] // TOKEN_TILE
    tile = lambda i: (i, 0)
    const = lambda i: (0, 0)
    return pl.pallas_call(
        functools.partial(_post_kernel, precise=False),
        grid=(n,),
        in_specs=[pl.BlockSpec((TOKEN_TILE, D_MODEL), tile),
                  pl.BlockSpec((D_MODEL, D_MODEL), const),
                  pl.BlockSpec((1, D_MODEL), const),
                  pl.BlockSpec((TOKEN_TILE, D_MODEL), tile),
                  pl.BlockSpec((1, D_MODEL), const),
                  pl.BlockSpec((D_MODEL, ROUTER_COLS), const),
                  pl.BlockSpec((1, ROUTER_COLS), const)],
        out_specs=[pl.BlockSpec((TOKEN_TILE, D_MODEL), tile),
                   pl.BlockSpec((TOKEN_TILE, D_MODEL), tile),
                   pl.BlockSpec((TOKEN_TILE, ROUTER_COLS), tile)],
        out_shape=[jax.ShapeDtypeStruct(x.shape, F32),
                   jax.ShapeDtypeStruct(x.shape, BF16),
                   jax.ShapeDtypeStruct((x.shape[0], ROUTER_COLS), F32)],
        compiler_params=_params("parallel"),
        name="mixer_out_route",
    )(a, w, b, x, g, wr, br)


def _moe_dense_kernel(h_ref, c_ref, wg_ref, wu_ref, wd_ref, x_ref, o_ref):
    e = pl.program_id(1)

    @pl.when(e == 0)
    def _():
        o_ref[...] = x_ref[...]

    h = h_ref[...]
    gate = jnp.dot(h, wg_ref[0], preferred_element_type=F32)
    up = jnp.dot(h, wu_ref[0], preferred_element_type=F32)
    hid = (gate * jax.nn.sigmoid(gate) * up).astype(BF16)
    y = jnp.dot(hid, wd_ref[0], preferred_element_type=F32)
    c = c_ref[...]
    lane = lax.broadcasted_iota(jnp.int32, c.shape, 1)
    c_e = jnp.sum(jnp.where(lane == e, c, 0.0), axis=-1, keepdims=True)
    o_ref[...] += c_e * y


def _moe_dense(h, c, wg, wu, wd, x):
    n = x.shape[0] // MOE_TOKEN_TILE
    tile = lambda i, e: (i, 0)
    wspec = lambda i, e: (e, 0, 0)
    return pl.pallas_call(
        _moe_dense_kernel,
        grid=(n, N_EXPERTS),
        in_specs=[pl.BlockSpec((MOE_TOKEN_TILE, D_MODEL), tile),
                  pl.BlockSpec((MOE_TOKEN_TILE, ROUTER_COLS), tile),
                  pl.BlockSpec((1, D_MODEL, D_EXPERT), wspec),
                  pl.BlockSpec((1, D_MODEL, D_EXPERT), wspec),
                  pl.BlockSpec((1, D_EXPERT, D_MODEL), wspec),
                  pl.BlockSpec((MOE_TOKEN_TILE, D_MODEL), tile)],
        out_specs=pl.BlockSpec((MOE_TOKEN_TILE, D_MODEL), tile),
        out_shape=jax.ShapeDtypeStruct(x.shape, F32),
        compiler_params=_params("parallel", "arbitrary"),
        name="moe_dense",
    )(h, c, wg, wu, wd, x)


def _glu(h, w_ref, b_ref):
    a = jnp.dot(h, w_ref[...], preferred_element_type=F32) + b_ref[...]
    return a[:, :D_MODEL] * jax.nn.sigmoid(a[:, D_MODEL:])


def _ln_swish(c, g, b):
    cc = c - jnp.mean(c, axis=-1, keepdims=True)
    var = jnp.mean(cc * cc, axis=-1, keepdims=True)
    y = cc * lax.rsqrt(var + EPS) * g + b
    return y * jax.nn.sigmoid(y)


def _conv_prompt_kernel(halo_ref, x_ref, g_ref, w1_ref, b1_ref, wdw_ref, bdw_ref, lng_ref, lnb_ref,
                        act_ref, tail_ref, u_scr):
    j = pl.program_id(1)
    xh = jnp.concatenate([halo_ref[...], x_ref[...]], axis=0)
    u = _glu(_rms(xh, g_ref[...]).astype(BF16), w1_ref, b1_ref)
    row = lax.broadcasted_iota(jnp.int32, (CONV_HALO + CONV_TILE, 1), 0)
    u_scr[...] = jnp.where((row < CONV_HALO) & (j == 0), 0.0, u)
    acc = jnp.zeros((CONV_TILE, D_MODEL), F32) + bdw_ref[...]
    for tap in range(CONV_WIDTH):
        off = CONV_HALO - HIST + tap
        acc = acc + wdw_ref[tap:tap + 1, :] * u_scr[off:off + CONV_TILE, :]
    act_ref[...] = _ln_swish(acc, lng_ref[...], lnb_ref[...]).astype(BF16)

    @pl.when(j == pl.num_programs(1) - 1)
    def _():
        tail_ref[0] = u_scr[CONV_TILE:CONV_TILE + CONV_HALO, :]


def _conv_prompt(x_all, g, w1, b1, wdw, bdw, lng, lnb):
    nt = SEQ // CONV_TILE
    per_halo = CONV_TILE // CONV_HALO
    const = lambda b, j: (0, 0)
    return pl.pallas_call(
        _conv_prompt_kernel,
        grid=(BATCH, nt),
        in_specs=[pl.BlockSpec((CONV_HALO, D_MODEL),
                               lambda b, j: (jnp.maximum((b * nt + j) * per_halo - 1, 0), 0)),
                  pl.BlockSpec((CONV_TILE, D_MODEL), lambda b, j: (b * nt + j, 0)),
                  pl.BlockSpec((1, D_MODEL), const),
                  pl.BlockSpec((D_MODEL, 2 * D_MODEL), const),
                  pl.BlockSpec((1, 2 * D_MODEL), const),
                  pl.BlockSpec((CONV_WIDTH, D_MODEL), const),
                  pl.BlockSpec((1, D_MODEL), const),
                  pl.BlockSpec((1, D_MODEL), const),
                  pl.BlockSpec((1, D_MODEL), const)],
        out_specs=[pl.BlockSpec((CONV_TILE, D_MODEL), lambda b, j: (b * nt + j, 0)),
                   pl.BlockSpec((1, CONV_HALO, D_MODEL), lambda b, j: (b, 0, 0))],
        out_shape=[jax.ShapeDtypeStruct((T_PROMPT, D_MODEL), BF16),
                   jax.ShapeDtypeStruct((BATCH, CONV_HALO, D_MODEL), F32)],
        scratch_shapes=[pltpu.VMEM((CONV_HALO + CONV_TILE, D_MODEL), F32)],
        compiler_params=_params("parallel", "arbitrary"),
        name="conv_prompt",
    )(x_all, x_all, g, w1, b1, wdw, bdw, lng, lnb)


SAMPLE_EXT = 40


def _conv_sample_kernel(x_ref, st_ref, g_ref, w1_ref, b1_ref, wsh_ref, bdw_ref, lng_ref, lnb_ref,
                        act_ref, u_ref):
    u = _glu(_rms(x_ref[...], g_ref[...]).astype(BF16), w1_ref, b1_ref)
    u_ref[...] = u
    pad = jnp.zeros((SAMPLE_EXT - HIST - DEC_SEQ, D_MODEL), F32)
    rows = []
    for b in range(SAMPLE_BATCH_BLOCK):
        ext = jnp.concatenate([st_ref[b], u[b * DEC_SEQ:(b + 1) * DEC_SEQ, :], pad], axis=0)
        for t in range(DEC_SEQ):
            rows.append(jnp.sum(wsh_ref[t] * ext, axis=0, keepdims=True))
    c = jnp.concatenate(rows, axis=0) + bdw_ref[...]
    act_ref[...] = _ln_swish(c, lng_ref[...], lnb_ref[...]).astype(BF16)


def _conv_sample(x_all, state, g, w1, b1, wsh, bdw, lng, lnb):
    bb = SAMPLE_BATCH_BLOCK
    rows = bb * DEC_SEQ
    first = T_PROMPT // rows
    const = lambda i: (0, 0)
    return pl.pallas_call(
        _conv_sample_kernel,
        grid=(DEC_BATCH // bb,),
        in_specs=[pl.BlockSpec((rows, D_MODEL), lambda i: (first + i, 0)),
                  pl.BlockSpec((bb, HIST, D_MODEL), lambda i: (i, 0, 0)),
                  pl.BlockSpec((1, D_MODEL), const),
                  pl.BlockSpec((D_MODEL, 2 * D_MODEL), const),
                  pl.BlockSpec((1, 2 * D_MODEL), const),
                  pl.BlockSpec((DEC_SEQ, SAMPLE_EXT, D_MODEL), lambda i: (0, 0, 0)),
                  pl.BlockSpec((1, D_MODEL), const),
                  pl.BlockSpec((1, D_MODEL), const),
                  pl.BlockSpec((1, D_MODEL), const)],
        out_specs=[pl.BlockSpec((rows, D_MODEL), lambda i: (i, 0)),
                   pl.BlockSpec((rows, D_MODEL), lambda i: (i, 0))],
        out_shape=[jax.ShapeDtypeStruct((T_SAMPLE, D_MODEL), BF16),
                   jax.ShapeDtypeStruct((T_SAMPLE, D_MODEL), F32)],
        compiler_params=_params("parallel"),
        name="conv_sample",
    )(x_all, state, g, w1, b1, wsh, bdw, lng, lnb)


def _final_norm_kernel(x_ref, g_ref, o_ref):
    o_ref[...] = _rms(x_ref[...], g_ref[...])


def _final_norm(x, g):
    n = x.shape[0] // TOKEN_TILE
    return pl.pallas_call(
        _final_norm_kernel,
        grid=(n,),
        in_specs=[pl.BlockSpec((TOKEN_TILE, D_MODEL), lambda i: (i, 0)),
                  pl.BlockSpec((1, D_MODEL), lambda i: (0, 0))],
        out_specs=pl.BlockSpec((TOKEN_TILE, D_MODEL), lambda i: (i, 0)),
        out_shape=jax.ShapeDtypeStruct(x.shape, F32),
        compiler_params=_params("parallel"),
        name="final_norm",
    )(x, g)


def _router_weights(w_gr, b_gr, w_er, b_er):
    w = jnp.zeros((D_MODEL, ROUTER_COLS), F32)
    w = w.at[:, :N_EXPERTS].set(w_er).at[:, N_EXPERTS:N_EXPERTS + N_GROUPS].set(w_gr)
    b = jnp.zeros((1, ROUTER_COLS), F32)
    b = b.at[0, :N_EXPERTS].set(b_er).at[0, N_EXPERTS:N_EXPERTS + N_GROUPS].set(b_gr)
    return w, b


def _shifted_taps(w_dw):
    frames = [jnp.pad(w_dw, ((t, SAMPLE_EXT - CONV_WIDTH - t), (0, 0))) for t in range(DEC_SEQ)]
    return jnp.stack(frames)


_NB = SEQ // BLOCK
_SAMPLE_BLOCKS = [T_PROMPT // BLOCK + i for i in range(T_SAMPLE // BLOCK)]
PRECISE_KV_BLOCKS = [b * _NB + i for b in range(BATCH) for i in (_NB - 2, _NB - 1)] + _SAMPLE_BLOCKS
PRECISE_Q_BLOCKS = [b * _NB + _NB - 1 for b in range(BATCH)] + _SAMPLE_BLOCKS
assert _NB >= 2 and WIN_BUF == BLOCK and T_SAMPLE % BLOCK == 0


def kernel(x_prompt, x_sample, cache_k, cache_v, state_conv, w_qkv, w_o, attn_sinks, w_pw1, b_pw1, w_dw, b_dw, conv_ln_g, conv_ln_b, w_pw2, b_pw2, norm_mix_g, norm_ffn_g, w_group_router, b_group_router, w_expert_router, b_expert_router, w_gate, w_up, w_down, final_norm_g):
    row = lambda v: v.reshape(1, -1)
    x = jnp.concatenate([x_prompt.reshape(T_PROMPT, D_MODEL), x_sample.reshape(T_SAMPLE, D_MODEL)], axis=0)
    outs = {}
    for layer in range(DEPTH):
        j = layer // 2
        g_ffn = row(norm_ffn_g[layer])
        wr, br = _router_weights(w_group_router[layer], b_group_router[layer],
                                 w_expert_router[layer], b_expert_router[layer])
        if layer % 2 == 0:
            g_mix = row(norm_mix_g[layer])
            ck = cache_k[j].reshape(DEC_BATCH, WIN_BUF, KV_DIM)
            cv = cache_v[j].reshape(DEC_BATCH, WIN_BUF, KV_DIM)
            q, kv = _qkv(x, g_mix, w_qkv[j].astype(BF16))
            o_p = _attn_prompt(attn_sinks[j], q, kv)
            o_s = _attn_sample(attn_sinks[j], q, kv, ck, cv, T_PROMPT, False)
            mixed = jnp.concatenate([o_p, o_s], axis=0)
            b_out = jnp.zeros((1, D_MODEL), F32)
            x1, h, comb = _post(mixed, w_o[j].astype(BF16), b_out, x, g_ffn, wr.astype(BF16), br)

            q_c, kv_c = _qkv_precise(jnp.asarray(PRECISE_KV_BLOCKS, jnp.int32), x, g_mix, w_qkv[j])
            n_prompt_rows = 2 * BATCH * BLOCK
            o_cp = _attn_prompt_last_precise(attn_sinks[j], q_c, kv_c)
            o_cs = _attn_sample(attn_sinks[j], q_c, kv_c, ck, cv, n_prompt_rows, True)
            x1, h, comb = _post_precise(jnp.asarray(PRECISE_Q_BLOCKS, jnp.int32),
                                        jnp.concatenate([o_cp, o_cs], axis=0), w_o[j], b_out, x, g_ffn,
                                        wr, br, x1, h, comb)

            kv_p = kv_c[:n_prompt_rows].reshape(BATCH, 2, BLOCK, 2 * KV_DIM)[:, 1]
            kv_s = kv_c[n_prompt_rows:].reshape(DEC_BATCH, DEC_SEQ, 2 * KV_DIM)
            shape5 = lambda a: a.reshape(a.shape[0], WIN_BUF, N_KV_HEADS, HEAD_DIM)
            outs.setdefault("k_p", []).append(shape5(kv_p[..., :KV_DIM]))
            outs.setdefault("v_p", []).append(shape5(kv_p[..., KV_DIM:]))
            outs.setdefault("k_s", []).append(shape5(jnp.concatenate([ck[:, DEC_SEQ:], kv_s[..., :KV_DIM]], axis=1)))
            outs.setdefault("v_s", []).append(shape5(jnp.concatenate([cv[:, DEC_SEQ:], kv_s[..., KV_DIM:]], axis=1)))
        else:
            g = row(norm_mix_g[layer])
            w1, b1 = w_pw1[j].astype(BF16), row(b_pw1[j])
            bdw, lng, lnb = row(b_dw[j]), row(conv_ln_g[j]), row(conv_ln_b[j])
            act_p, tail = _conv_prompt(x, g, w1, b1, w_dw[j], bdw, lng, lnb)
            act_s, u_s = _conv_sample(x, state_conv[j], g, w1, b1, _shifted_taps(w_dw[j]), bdw, lng, lnb)
            mixed = jnp.concatenate([act_p, act_s], axis=0)
            x1, h, comb = _post(mixed, w_pw2[j].astype(BF16), row(b_pw2[j]), x, g_ffn, wr.astype(BF16), br)
            outs.setdefault("c_p", []).append(tail[:, CONV_HALO - HIST:])
            u_s = u_s.reshape(DEC_BATCH, DEC_SEQ, D_MODEL)
            outs.setdefault("c_s", []).append(jnp.concatenate([state_conv[j][:, DEC_SEQ:], u_s], axis=1))
        x = _moe_dense(h, comb, w_gate[layer].astype(BF16), w_up[layer].astype(BF16),
                       w_down[layer].astype(BF16), x1)
    y = _final_norm(x, row(final_norm_g))
    y_prompt = y[:T_PROMPT].reshape(BATCH, SEQ, D_MODEL)
    y_sample = y[T_PROMPT:].reshape(DEC_BATCH, DEC_SEQ, D_MODEL)
    return (y_prompt, y_sample, jnp.stack(outs["k_p"]), jnp.stack(outs["v_p"]), jnp.stack(outs["c_p"]),
            jnp.stack(outs["k_s"]), jnp.stack(outs["v_s"]), jnp.stack(outs["c_s"]))
```

```python
import functools

import jax
import jax.numpy as jnp
from jax import lax
from jax.experimental import pallas as pl
from jax.experimental.pallas import tpu as pltpu

D_MODEL = 1024
BATCH = 4
SEQ = 4096
DEPTH = 2
DEC_BATCH = 128
DEC_SEQ = 4
HEAD_DIM = 64
N_HEADS = 16
N_KV_HEADS = 4
GQA_GROUP = 4
WINDOW = 128
BLOCK = 128
WIN_BUF = 128
CONV_WIDTH = 31
HIST = CONV_WIDTH - 1
N_GROUPS = 4
EXPERTS_PER_GROUP = 8
N_EXPERTS = 32
D_EXPERT = 256
EPS = 1e-6
NEG_INF = -1e30

T_PROMPT = BATCH * SEQ
T_SAMPLE = DEC_BATCH * DEC_SEQ
T_ALL = T_PROMPT + T_SAMPLE
KV_DIM = N_KV_HEADS * HEAD_DIM
QKV_OUT = D_MODEL + 2 * KV_DIM

LANES = 128
TOKEN_TILE = 512
MOE_TOKEN_TILE = 1536
CONV_TILE = 512
CONV_HALO = 32
SAMPLE_BATCH_BLOCK = 8
VMEM_LIMIT = 48 * 1024 * 1024

F32 = jnp.float32
BF16 = jnp.bfloat16


def _params(*sem):
    return pltpu.CompilerParams(dimension_semantics=sem, vmem_limit_bytes=VMEM_LIMIT)


def _rms(x, g):
    return x * lax.rsqrt(jnp.mean(x * x, axis=-1, keepdims=True) + EPS) * g


def _mm(a, b, precise, contract_b=0):
    dims = (((1,), (contract_b,)), ((), ()))
    if precise:
        return lax.dot_general(a.astype(F32), b.astype(F32), dims, precision=lax.Precision.HIGHEST,
                               preferred_element_type=F32)
    return lax.dot_general(a.astype(BF16), b.astype(BF16), dims, preferred_element_type=F32)


def _qkv_kernel(x_ref, g_ref, w_ref, q_ref, kv_ref, *, precise):
    qkv = _mm(_rms(x_ref[...], g_ref[...]), w_ref[...], precise)
    q_ref[...] = (qkv[:, :D_MODEL] * (HEAD_DIM ** -0.5)).astype(q_ref.dtype)
    kv_ref[...] = qkv[:, D_MODEL:]


def _qkv_precise_kernel(tbl_ref, x_ref, g_ref, w_ref, q_ref, kv_ref):
    del tbl_ref
    _qkv_kernel(x_ref, g_ref, w_ref, q_ref, kv_ref, precise=True)


def _qkv_precise(blocks, x, g, w):
    n = blocks.shape[0]
    return pl.pallas_call(
        _qkv_precise_kernel,
        grid_spec=pltpu.PrefetchScalarGridSpec(
            num_scalar_prefetch=1,
            grid=(n,),
            in_specs=[pl.BlockSpec((BLOCK, D_MODEL), lambda i, t: (t[i], 0)),
                      pl.BlockSpec((1, D_MODEL), lambda i, t: (0, 0)),
                      pl.BlockSpec((D_MODEL, QKV_OUT), lambda i, t: (0, 0))],
            out_specs=[pl.BlockSpec((BLOCK, D_MODEL), lambda i, t: (i, 0)),
                       pl.BlockSpec((BLOCK, 2 * KV_DIM), lambda i, t: (i, 0))],
        ),
        out_shape=[jax.ShapeDtypeStruct((n * BLOCK, D_MODEL), F32),
                   jax.ShapeDtypeStruct((n * BLOCK, 2 * KV_DIM), F32)],
        compiler_params=_params("arbitrary"),
        name="qkv_proj_precise",
    )(blocks, x, g, w)


def _qkv(x, g, w):
    n = x.shape[0] // TOKEN_TILE
    return pl.pallas_call(
        functools.partial(_qkv_kernel, precise=False),
        grid=(n,),
        in_specs=[pl.BlockSpec((TOKEN_TILE, D_MODEL), lambda i: (i, 0)),
                  pl.BlockSpec((1, D_MODEL), lambda i: (0, 0)),
                  pl.BlockSpec((D_MODEL, QKV_OUT), lambda i: (0, 0))],
        out_specs=[pl.BlockSpec((TOKEN_TILE, D_MODEL), lambda i: (i, 0)),
                   pl.BlockSpec((TOKEN_TILE, 2 * KV_DIM), lambda i: (i, 0))],
        out_shape=[jax.ShapeDtypeStruct((x.shape[0], D_MODEL), BF16),
                   jax.ShapeDtypeStruct((x.shape[0], 2 * KV_DIM), F32)],
        compiler_params=_params("parallel"),
        name="qkv_proj",
    )(x, g, w)


def _alibi_slope(head):
    return 2.0 ** (-8.0 * (head + 1) / N_HEADS)


def _softmax_pv(s, dist_f, valid, slope, sink, v, precise):
    logits = jnp.where(valid, s - slope * dist_f, NEG_INF)
    m = jnp.maximum(jnp.max(logits, axis=-1, keepdims=True), sink)
    p = jnp.exp(logits - m)
    denom = jnp.sum(p, axis=-1, keepdims=True) + jnp.exp(sink - m)
    return _mm(p, v, precise) / denom


def _attn_prompt_kernel(sink_ref, q_ref, kvp_ref, kvc_ref, o_ref, *, precise, first_block):
    q = q_ref[...]
    kv = jnp.concatenate([kvp_ref[...], kvc_ref[...]], axis=0)
    kv = kv if precise else kv.astype(BF16)
    qi = lax.broadcasted_iota(jnp.int32, (BLOCK, 2 * BLOCK), 0)
    kj = lax.broadcasted_iota(jnp.int32, (BLOCK, 2 * BLOCK), 1)
    dist = qi + BLOCK - kj
    valid = (dist >= 0) & (dist < WINDOW) & ((kj >= BLOCK) | jnp.logical_not(first_block()))
    dist_f = dist.astype(F32)
    for kvh in range(N_KV_HEADS):
        k_h = kv[:, kvh * HEAD_DIM:(kvh + 1) * HEAD_DIM]
        v_h = kv[:, KV_DIM + kvh * HEAD_DIM:KV_DIM + (kvh + 1) * HEAD_DIM]
        for g in range(GQA_GROUP):
            head = kvh * GQA_GROUP + g
            q_h = q[:, head * HEAD_DIM:(head + 1) * HEAD_DIM]
            s = _mm(q_h, k_h, precise, contract_b=1)
            o = _softmax_pv(s, dist_f, valid, _alibi_slope(head), sink_ref[head], v_h, precise)
            o_ref[:, head * HEAD_DIM:(head + 1) * HEAD_DIM] = o.astype(o_ref.dtype)


def _attn_prompt_last_precise(sinks, q_c, kv_c):
    return pl.pallas_call(
        functools.partial(_attn_prompt_kernel, precise=True, first_block=lambda: SEQ // BLOCK == 1),
        grid_spec=pltpu.PrefetchScalarGridSpec(
            num_scalar_prefetch=1,
            grid=(BATCH,),
            in_specs=[pl.BlockSpec((BLOCK, D_MODEL), lambda b, s: (2 * b + 1, 0)),
                      pl.BlockSpec((BLOCK, 2 * KV_DIM), lambda b, s: (2 * b, 0)),
                      pl.BlockSpec((BLOCK, 2 * KV_DIM), lambda b, s: (2 * b + 1, 0))],
            out_specs=pl.BlockSpec((BLOCK, D_MODEL), lambda b, s: (b, 0)),
        ),
        out_shape=jax.ShapeDtypeStruct((BATCH * BLOCK, D_MODEL), F32),
        compiler_params=_params("arbitrary"),
        name="attn_prompt_precise",
    )(sinks, q_c, kv_c, kv_c)


def _attn_prompt(sinks, q_all, kv_all):
    nb = SEQ // BLOCK
    return pl.pallas_call(
        functools.partial(_attn_prompt_kernel, precise=False, first_block=lambda: pl.program_id(1) == 0),
        grid_spec=pltpu.PrefetchScalarGridSpec(
            num_scalar_prefetch=1,
            grid=(BATCH, nb),
            in_specs=[pl.BlockSpec((BLOCK, D_MODEL), lambda b, i, s: (b * nb + i, 0)),
                      pl.BlockSpec((BLOCK, 2 * KV_DIM), lambda b, i, s: (b * nb + jnp.maximum(i - 1, 0), 0)),
                      pl.BlockSpec((BLOCK, 2 * KV_DIM), lambda b, i, s: (b * nb + i, 0))],
            out_specs=pl.BlockSpec((BLOCK, D_MODEL), lambda b, i, s: (b * nb + i, 0)),
        ),
        out_shape=jax.ShapeDtypeStruct((T_PROMPT, D_MODEL), BF16),
        compiler_params=_params("parallel", "parallel"),
        name="attn_prompt",
    )(sinks, q_all, kv_all, kv_all)


SAMPLE_KEYS = WIN_BUF + 16


def _attn_sample_kernel(sink_ref, q_ref, kvn_ref, ck_ref, cv_ref, o_ref, *, precise):
    rows = GQA_GROUP * DEC_SEQ
    t_q = lax.broadcasted_iota(jnp.int32, (rows, SAMPLE_KEYS), 0) % DEC_SEQ
    s_k = lax.broadcasted_iota(jnp.int32, (rows, SAMPLE_KEYS), 1)
    dist = WIN_BUF + t_q - s_k
    valid = (dist >= 0) & (dist < WINDOW) & (s_k < WIN_BUF + DEC_SEQ)
    dist_f = dist.astype(F32)
    g_row = lax.broadcasted_iota(jnp.int32, (rows, 1), 0) // DEC_SEQ
    pad = jnp.zeros((SAMPLE_KEYS - WIN_BUF - DEC_SEQ, KV_DIM), F32)
    for b in range(SAMPLE_BATCH_BLOCK):
        r0 = b * DEC_SEQ
        q_b = q_ref[r0:r0 + DEC_SEQ, :]
        kvn = kvn_ref[r0:r0 + DEC_SEQ, :]
        k_all = jnp.concatenate([ck_ref[b], kvn[:, :KV_DIM], pad], axis=0)
        v_all = jnp.concatenate([cv_ref[b], kvn[:, KV_DIM:], pad], axis=0)
        if not precise:
            k_all, v_all = k_all.astype(BF16), v_all.astype(BF16)
        for kvh in range(N_KV_HEADS):
            k_h = k_all[:, kvh * HEAD_DIM:(kvh + 1) * HEAD_DIM]
            v_h = v_all[:, kvh * HEAD_DIM:(kvh + 1) * HEAD_DIM]
            q_h = jnp.concatenate(
                [q_b[:, (kvh * GQA_GROUP + g) * HEAD_DIM:(kvh * GQA_GROUP + g + 1) * HEAD_DIM]
                 for g in range(GQA_GROUP)], axis=0)
            s = _mm(q_h, k_h, precise, contract_b=1)
            slope = jnp.zeros((rows, 1), F32)
            sink = jnp.zeros((rows, 1), F32)
            for g in range(GQA_GROUP):
                head = kvh * GQA_GROUP + g
                slope = jnp.where(g_row == g, _alibi_slope(head), slope)
                sink = jnp.where(g_row == g, sink_ref[head], sink)
            o = _softmax_pv(s, dist_f, valid, slope, sink, v_h, precise).astype(o_ref.dtype)
            for g in range(GQA_GROUP):
                head = kvh * GQA_GROUP + g
                o_ref[r0:r0 + DEC_SEQ, head * HEAD_DIM:(head + 1) * HEAD_DIM] = (
                    o[g * DEC_SEQ:(g + 1) * DEC_SEQ, :])


def _attn_sample(sinks, q_all, kv_all, cache_k, cache_v, first_row, precise):
    bb = SAMPLE_BATCH_BLOCK
    rows = bb * DEC_SEQ
    first = first_row // rows
    return pl.pallas_call(
        functools.partial(_attn_sample_kernel, precise=precise),
        grid_spec=pltpu.PrefetchScalarGridSpec(
            num_scalar_prefetch=1,
            grid=(DEC_BATCH // bb,),
            in_specs=[pl.BlockSpec((rows, D_MODEL), lambda i, s: (first + i, 0)),
                      pl.BlockSpec((rows, 2 * KV_DIM), lambda i, s: (first + i, 0)),
                      pl.BlockSpec((bb, WIN_BUF, KV_DIM), lambda i, s: (i, 0, 0)),
                      pl.BlockSpec((bb, WIN_BUF, KV_DIM), lambda i, s: (i, 0, 0))],
            out_specs=pl.BlockSpec((rows, D_MODEL), lambda i, s: (i, 0)),
        ),
        out_shape=jax.ShapeDtypeStruct((T_SAMPLE, D_MODEL), F32 if precise else BF16),
        compiler_params=_params("parallel"),
        name="attn_sample_precise" if precise else "attn_sample",
    )(sinks, q_all, kv_all, cache_k, cache_v)


ROUTER_COLS = LANES


def _route(logits):
    lane = lax.broadcasted_iota(jnp.int32, logits.shape, 1)
    is_group = (lane >= N_EXPERTS) & (lane < N_EXPERTS + N_GROUPS)
    gl = jnp.where(is_group, logits, -jnp.inf)
    g_max = jnp.max(gl, axis=-1, keepdims=True)
    g_idx = jnp.min(jnp.where(gl == g_max, lane - N_EXPERTS, N_GROUPS), axis=-1, keepdims=True)
    g_w = 1.0 / jnp.sum(jnp.exp(gl - g_max), axis=-1, keepdims=True)
    in_group = (lane < N_EXPERTS) & ((lane // EXPERTS_PER_GROUP) == g_idx)
    el = jnp.where(in_group, logits, -jnp.inf)
    m1 = jnp.max(el, axis=-1, keepdims=True)
    i1 = jnp.min(jnp.where(el == m1, lane, ROUTER_COLS), axis=-1, keepdims=True)
    el2 = jnp.where(lane == i1, -jnp.inf, el)
    m2 = jnp.max(el2, axis=-1, keepdims=True)
    i2 = jnp.min(jnp.where(el2 == m2, lane, ROUTER_COLS), axis=-1, keepdims=True)
    r = jnp.exp(m2 - m1)
    w1 = g_w / (1.0 + r)
    w2 = g_w * r / (1.0 + r)
    return i1, i2, w1, w2


MOE_TILE = TOKEN_TILE
ROW_ALIGN = 16
SLOTS = 2 * MOE_TILE + N_EXPERTS * ROW_ALIGN
X_COLS = D_MODEL + LANES
N_TILES = T_ALL // MOE_TILE
EXPERT_ROWS = 512
N_SORTED = N_TILES * SLOTS + N_EXPERTS * EXPERT_ROWS
MAX_ROW_TILES = N_SORTED // EXPERT_ROWS
N_CHUNKS = N_TILES * N_EXPERTS


def _split3(w):
    hi = w.astype(BF16)
    r = w - hi.astype(F32)
    mid = r.astype(BF16)
    lo = (r - mid.astype(F32)).astype(BF16)
    return hi.astype(F32), mid.astype(F32), lo.astype(F32)


def _lane_pack(lane, cols):
    out = jnp.zeros(lane.shape, F32)
    for k, c in enumerate(cols):
        out = jnp.where(lane == k, c, out)
    return out


def _post_kernel(*refs, patched):
    if patched:
        (a_ref, w_ref, b_ref, x_ref, g_ref, wr_ref, br_ref, px1_ref, plg_ref,
         x1_ref, xs_ref, meta_ref, cnt_ref) = refs
    else:
        a_ref, w_ref, b_ref, x_ref, g_ref, wr_ref, br_ref, x1_ref, xs_ref, meta_ref, cnt_ref = refs
    tile = pl.program_id(0)
    x1 = x_ref[...] + (_mm(a_ref[...], w_ref[...], False) + b_ref[...])
    if patched:
        row = lax.broadcasted_iota(jnp.int32, (MOE_TILE, 1), 0)
        tiles_per_seq = SEQ // MOE_TILE
        use = (tile >= T_PROMPT // MOE_TILE) | (
            (tile % tiles_per_seq == tiles_per_seq - 1) & (row >= MOE_TILE - BLOCK))
        x1 = jnp.where(use, px1_ref[...], x1)
    x1_ref[...] = x1
    h = _rms(x1, g_ref[...])
    logits = _mm(h, wr_ref[...], False) + br_ref[...]
    if patched:
        logits = jnp.where(use, plg_ref[...], logits)
    i1, i2, w1, w2 = _route(logits)

    lane = lax.broadcasted_iota(jnp.int32, (MOE_TILE, LANES), 1)
    a1, a2 = lane == i1, lane == i2
    assigned = jnp.where(a1 | a2, 1.0, 0.0).astype(BF16)
    r_t = lax.broadcasted_iota(jnp.int32, (MOE_TILE, MOE_TILE), 0)
    c_t = lax.broadcasted_iota(jnp.int32, (MOE_TILE, MOE_TILE), 1)
    before = jnp.where(c_t < r_t, 1.0, 0.0).astype(BF16)
    rank = jnp.dot(before, assigned, preferred_element_type=F32)
    count = rank[MOE_TILE - 1:, :] + assigned[MOE_TILE - 1:, :].astype(F32)
    padded = jnp.floor((count + (ROW_ALIGN - 1)) * (1.0 / ROW_ALIGN)) * ROW_ALIGN
    r_e = lax.broadcasted_iota(jnp.int32, (LANES, LANES), 0)
    c_e = lax.broadcasted_iota(jnp.int32, (LANES, LANES), 1)
    lower_experts = jnp.where(r_e < c_e, 1.0, 0.0).astype(BF16)
    start = jnp.dot(jnp.broadcast_to(padded, (8, LANES)).astype(BF16), lower_experts,
                    preferred_element_type=F32)[0:1, :]
    pos = start + rank
    s1 = jnp.sum(jnp.where(a1, pos, 0.0), axis=-1, keepdims=True)
    s2 = jnp.sum(jnp.where(a2, pos, 0.0), axis=-1, keepdims=True)
    meta = _lane_pack(lane, [s1, s2])
    meta_ref[...] = meta
    cnt_ref[0] = jnp.broadcast_to(padded, (8, LANES)).astype(jnp.int32)

    slots_t = meta.T
    s_iota = lax.broadcasted_iota(jnp.int32, (SLOTS, MOE_TILE), 0)
    p1 = jnp.where(s_iota == slots_t[0:1, :].astype(jnp.int32), 1.0, 0.0).astype(BF16)
    p2 = jnp.where(s_iota == slots_t[1:2, :].astype(jnp.int32), 1.0, 0.0).astype(BF16)
    xs_ref[:, :D_MODEL] = jnp.dot(p1 + p2, h.astype(BF16), preferred_element_type=F32).astype(BF16)
    wcols = (jnp.dot(p1, _lane_pack(lane, _split3(w1)).astype(BF16), preferred_element_type=F32)
             + jnp.dot(p2, _lane_pack(lane, _split3(w2)).astype(BF16), preferred_element_type=F32))
    xs_ref[:, D_MODEL:] = wcols.astype(BF16)


def _post(a, w, b, x, g, wr, br, patch=None):
    tile = lambda i: (i, 0)
    const = lambda i: (0, 0)
    in_specs = [pl.BlockSpec((MOE_TILE, D_MODEL), tile),
                pl.BlockSpec((D_MODEL, D_MODEL), const),
                pl.BlockSpec((1, D_MODEL), const),
                pl.BlockSpec((MOE_TILE, D_MODEL), tile),
                pl.BlockSpec((1, D_MODEL), const),
                pl.BlockSpec((D_MODEL, ROUTER_COLS), const),
                pl.BlockSpec((1, ROUTER_COLS), const)]
    args = [a, w, b, x, g, wr, br]
    if patch is not None:
        tiles_per_seq = SEQ // MOE_TILE
        pidx = lambda i: (jnp.where(i >= BATCH * tiles_per_seq, i - BATCH * (tiles_per_seq - 1),
                                    i // tiles_per_seq), 0)
        in_specs += [pl.BlockSpec((MOE_TILE, D_MODEL), pidx), pl.BlockSpec((MOE_TILE, ROUTER_COLS), pidx)]
        args += list(patch)
    return pl.pallas_call(
        functools.partial(_post_kernel, patched=patch is not None),
        grid=(N_TILES,),
        in_specs=in_specs,
        out_specs=[pl.BlockSpec((MOE_TILE, D_MODEL), tile),
                   pl.BlockSpec((SLOTS, X_COLS), tile),
                   pl.BlockSpec((MOE_TILE, LANES), tile),
                   pl.BlockSpec((1, 8, LANES), lambda i: (i, 0, 0))],
        out_shape=[jax.ShapeDtypeStruct(x.shape, F32),
                   jax.ShapeDtypeStruct((N_TILES * SLOTS, X_COLS), BF16),
                   jax.ShapeDtypeStruct((x.shape[0], LANES), F32),
                   jax.ShapeDtypeStruct((N_TILES, 8, LANES), jnp.int32)],
        compiler_params=_params("parallel"),
        name="mixer_out_route_sort",
    )(*args)


def _chunk_plan(counts):
    tile_start = jnp.cumsum(counts, axis=1) - counts
    tile_rows = jnp.sum(counts, axis=1)
    seg_rows = jnp.sum(counts, axis=0)
    seg_pad = (seg_rows + EXPERT_ROWS - 1) // EXPERT_ROWS * EXPERT_ROWS
    seg_end = jnp.cumsum(seg_pad)
    seg_start = seg_end - seg_pad
    local = jnp.arange(N_TILES, dtype=jnp.int32)[:, None] * SLOTS + tile_start
    sorted_ = seg_start[None, :] + jnp.cumsum(counts, axis=0) - counts
    n_row_tiles = seg_end[-1] // EXPERT_ROWS
    first_row = jnp.arange(MAX_ROW_TILES, dtype=jnp.int32) * EXPERT_ROWS
    tile_expert = jnp.minimum(jnp.sum(first_row[:, None] >= seg_end[None, :], axis=1), N_EXPERTS - 1)
    i32 = lambda v: v.astype(jnp.int32)
    return dict(local=i32(local.reshape(-1)), sorted=i32(sorted_.reshape(-1)), rows=i32(counts.reshape(-1)),
                tile_rows=i32(tile_rows), tile_end=i32(jnp.arange(N_TILES) * SLOTS + tile_rows),
                seg_fill=i32(seg_start + seg_rows), seg_gap=i32(seg_pad - seg_rows),
                moved=i32(jnp.sum(counts)).reshape(1), n_row_tiles=i32(n_row_tiles).reshape(1),
                tile_expert=i32(tile_expert))


def _regroup_kernel(src_ref, dst_ref, rows_ref, fdst_ref, frows_ref, total_ref,
                    data_ref, zeros_ref, out_ref, sem):
    def copy(c, carry):
        n = pl.multiple_of(rows_ref[c], ROW_ALIGN)

        @pl.when(n > 0)
        def _():
            pltpu.make_async_copy(data_ref.at[pl.ds(pl.multiple_of(src_ref[c], ROW_ALIGN), n)],
                                  out_ref.at[pl.ds(pl.multiple_of(dst_ref[c], ROW_ALIGN), n)], sem).start()
        return carry

    def fill(c, carry):
        n = pl.multiple_of(frows_ref[c], ROW_ALIGN)

        @pl.when(n > 0)
        def _():
            pltpu.make_async_copy(zeros_ref.at[pl.ds(0, n)],
                                  out_ref.at[pl.ds(pl.multiple_of(fdst_ref[c], ROW_ALIGN), n)], sem).start()
        return carry

    lax.fori_loop(0, src_ref.shape[0], copy, 0)
    lax.fori_loop(0, fdst_ref.shape[0], fill, 0)
    total = pl.multiple_of(total_ref[0], ROW_ALIGN)

    @pl.when(total > 0)
    def _():
        pltpu.make_async_copy(out_ref.at[pl.ds(0, total)], out_ref.at[pl.ds(0, total)], sem).wait()


def _regroup(src, dst, rows, fill_dst, fill_rows, data, out_rows, name):
    cols = data.shape[1]
    zeros = jnp.zeros((SLOTS, cols), data.dtype)
    total = (jnp.sum(rows) + jnp.sum(fill_rows)).astype(jnp.int32).reshape(1)
    any_space = pl.BlockSpec(memory_space=pl.ANY)
    return pl.pallas_call(
        _regroup_kernel,
        grid_spec=pltpu.PrefetchScalarGridSpec(
            num_scalar_prefetch=6,
            grid=(1,),
            in_specs=[any_space, any_space],
            out_specs=any_space,
            scratch_shapes=[pltpu.SemaphoreType.DMA(())],
        ),
        out_shape=jax.ShapeDtypeStruct((out_rows, cols), data.dtype),
        compiler_params=_params("arbitrary"),
        name=name,
    )(src, dst, rows, fill_dst, fill_rows, total, data, zeros)


def _expert_kernel(te_ref, nt_ref, x_ref, wg_ref, wu_ref, wd_ref, y_ref, wgu_scr, wd_scr):
    r = pl.program_id(0)

    @pl.when(r < nt_ref[0])
    def _():
        @pl.when((r == 0) | (te_ref[r] != te_ref[jnp.maximum(r - 1, 0)]))
        def _():
            wgu_scr[:, :D_EXPERT] = wg_ref[0].astype(BF16)
            wgu_scr[:, D_EXPERT:] = wu_ref[0].astype(BF16)
            wd_scr[...] = wd_ref[0].astype(BF16)

        gu = jnp.dot(x_ref[:, :D_MODEL], wgu_scr[...], preferred_element_type=F32)
        gate, up = gu[:, :D_EXPERT], gu[:, D_EXPERT:]
        hid = (gate * jax.nn.sigmoid(gate) * up).astype(BF16)
        y = jnp.dot(hid, wd_scr[...], preferred_element_type=F32)
        wparts = x_ref[:, D_MODEL:].astype(F32)
        weight = wparts[:, 0:1] + wparts[:, 1:2] + wparts[:, 2:3]
        y_ref[...] = (weight * y).astype(BF16)


def _experts(tile_expert, n_row_tiles, xg, wg, wu, wd):
    row = lambda r, te, nt: (jnp.minimum(r, nt[0] - 1), 0)
    wsel = lambda r, te, nt: (te[jnp.minimum(r, nt[0] - 1)], 0, 0)
    return pl.pallas_call(
        _expert_kernel,
        grid_spec=pltpu.PrefetchScalarGridSpec(
            num_scalar_prefetch=2,
            grid=(MAX_ROW_TILES,),
            in_specs=[pl.BlockSpec((EXPERT_ROWS, X_COLS), row),
                      pl.BlockSpec((1, D_MODEL, D_EXPERT), wsel),
                      pl.BlockSpec((1, D_MODEL, D_EXPERT), wsel),
                      pl.BlockSpec((1, D_EXPERT, D_MODEL), wsel)],
            out_specs=pl.BlockSpec((EXPERT_ROWS, D_MODEL), row),
            scratch_shapes=[pltpu.VMEM((D_MODEL, 2 * D_EXPERT), BF16),
                            pltpu.VMEM((D_EXPERT, D_MODEL), BF16)],
        ),
        out_shape=jax.ShapeDtypeStruct((N_SORTED, D_MODEL), BF16),
        compiler_params=_params("arbitrary"),
        name="experts",
    )(tile_expert, n_row_tiles, xg, wg, wu, wd)


def _combine_kernel(y_ref, meta_ref, x1_ref, g_ref, o_ref, *, final):
    meta = meta_ref[...]
    s_iota = lax.broadcasted_iota(jnp.int32, (MOE_TILE, SLOTS), 1)
    pick = ((s_iota == meta[:, 0:1].astype(jnp.int32)) | (s_iota == meta[:, 1:2].astype(jnp.int32)))
    moe = jnp.dot(jnp.where(pick, 1.0, 0.0).astype(BF16), y_ref[...], preferred_element_type=F32)
    x2 = x1_ref[...] + moe
    o_ref[...] = _rms(x2, g_ref[...]) if final else x2


def _combine(y_local, meta, x1, g_final, final):
    tile = lambda i: (i, 0)
    return pl.pallas_call(
        functools.partial(_combine_kernel, final=final),
        grid=(N_TILES,),
        in_specs=[pl.BlockSpec((SLOTS, D_MODEL), tile),
                  pl.BlockSpec((MOE_TILE, LANES), tile),
                  pl.BlockSpec((MOE_TILE, D_MODEL), tile),
                  pl.BlockSpec((1, D_MODEL), lambda i: (0, 0))],
        out_specs=pl.BlockSpec((MOE_TILE, D_MODEL), tile),
        out_shape=jax.ShapeDtypeStruct(x1.shape, F32),
        compiler_params=_params("parallel"),
        name="moe_combine",
    )(y_local, meta, x1, g_final)


def _moe(xs, meta, counts, x1, wg, wu, wd, g_final, final):
    plan = _chunk_plan(counts[:, 0, :N_EXPERTS])
    xg = _regroup(plan["local"], plan["sorted"], plan["rows"], plan["seg_fill"], plan["seg_gap"],
                  xs, N_SORTED, "dispatch")
    yg = _experts(plan["tile_expert"], plan["n_row_tiles"], xg, wg, wu, wd)
    y_local = _regroup(plan["sorted"], plan["local"], plan["rows"], plan["tile_end"],
                       SLOTS - plan["tile_rows"], yg, N_TILES * SLOTS, "undispatch")
    return _combine(y_local, meta, x1, g_final, final)


def _post_precise_kernel(src_ref, tok_ref, a_ref, w_ref, b_ref, x_ref, g_ref, wr_ref, br_ref, x1_ref, lg_ref):
    del tok_ref
    filled = src_ref[pl.program_id(0)] >= 0

    @pl.when(filled)
    def _():
        x1 = x_ref[...] + (_mm(a_ref[...], w_ref[...], True) + b_ref[...])
        x1_ref[...] = x1
        lg_ref[...] = _mm(_rms(x1, g_ref[...]), wr_ref[...], True) + br_ref[...]

    @pl.when(jnp.logical_not(filled))
    def _():
        x1_ref[...] = jnp.zeros_like(x1_ref)
        lg_ref[...] = jnp.zeros_like(lg_ref)


def _post_precise(src_blocks, tok_blocks, a_c, w, b, x, g, wr, br):
    n = src_blocks.shape[0]
    const = lambda i, s, t: (0, 0)
    slot = lambda i, s, t: (i, 0)
    return pl.pallas_call(
        _post_precise_kernel,
        grid_spec=pltpu.PrefetchScalarGridSpec(
            num_scalar_prefetch=2,
            grid=(n,),
            in_specs=[pl.BlockSpec((BLOCK, D_MODEL), lambda i, s, t: (jnp.maximum(s[i], 0), 0)),
                      pl.BlockSpec((D_MODEL, D_MODEL), const),
                      pl.BlockSpec((1, D_MODEL), const),
                      pl.BlockSpec((BLOCK, D_MODEL), lambda i, s, t: (t[i], 0)),
                      pl.BlockSpec((1, D_MODEL), const),
                      pl.BlockSpec((D_MODEL, ROUTER_COLS), const),
                      pl.BlockSpec((1, ROUTER_COLS), const)],
            out_specs=[pl.BlockSpec((BLOCK, D_MODEL), slot),
                       pl.BlockSpec((BLOCK, ROUTER_COLS), slot)],
        ),
        out_shape=[jax.ShapeDtypeStruct((n * BLOCK, D_MODEL), F32),
                   jax.ShapeDtypeStruct((n * BLOCK, ROUTER_COLS), F32)],
        compiler_params=_params("arbitrary"),
        name="mixer_out_route_precise",
    )(src_blocks, tok_blocks, a_c, w, b, x, g, wr, br)


def _glu(h, w_ref, b_ref):
    a = jnp.dot(h, w_ref[...], preferred_element_type=F32) + b_ref[...]
    return a[:, :D_MODEL] * jax.nn.sigmoid(a[:, D_MODEL:])


def _ln_swish(c, g, b):
    cc = c - jnp.mean(c, axis=-1, keepdims=True)
    var = jnp.mean(cc * cc, axis=-1, keepdims=True)
    y = cc * lax.rsqrt(var + EPS) * g + b
    return y * jax.nn.sigmoid(y)


def _conv_prompt_kernel(halo_ref, x_ref, g_ref, w1_ref, b1_ref, wdw_ref, bdw_ref, lng_ref, lnb_ref,
                        act_ref, tail_ref, u_scr):
    j = pl.program_id(1)
    xh = jnp.concatenate([halo_ref[...], x_ref[...]], axis=0)
    u = _glu(_rms(xh, g_ref[...]).astype(BF16), w1_ref, b1_ref)
    row = lax.broadcasted_iota(jnp.int32, (CONV_HALO + CONV_TILE, 1), 0)
    u_scr[...] = jnp.where((row < CONV_HALO) & (j == 0), 0.0, u)
    acc = jnp.zeros((CONV_TILE, D_MODEL), F32) + bdw_ref[...]
    for tap in range(CONV_WIDTH):
        off = CONV_HALO - HIST + tap
        acc = acc + wdw_ref[tap:tap + 1, :] * u_scr[off:off + CONV_TILE, :]
    act_ref[...] = _ln_swish(acc, lng_ref[...], lnb_ref[...]).astype(BF16)

    @pl.when(j == pl.num_programs(1) - 1)
    def _():
        tail_ref[0] = u_scr[CONV_TILE:CONV_TILE + CONV_HALO, :]


def _conv_prompt(x_all, g, w1, b1, wdw, bdw, lng, lnb):
    nt = SEQ // CONV_TILE
    per_halo = CONV_TILE // CONV_HALO
    const = lambda b, j: (0, 0)
    return pl.pallas_call(
        _conv_prompt_kernel,
        grid=(BATCH, nt),
        in_specs=[pl.BlockSpec((CONV_HALO, D_MODEL),
                               lambda b, j: (jnp.maximum((b * nt + j) * per_halo - 1, 0), 0)),
                  pl.BlockSpec((CONV_TILE, D_MODEL), lambda b, j: (b * nt + j, 0)),
                  pl.BlockSpec((1, D_MODEL), const),
                  pl.BlockSpec((D_MODEL, 2 * D_MODEL), const),
                  pl.BlockSpec((1, 2 * D_MODEL), const),
                  pl.BlockSpec((CONV_WIDTH, D_MODEL), const),
                  pl.BlockSpec((1, D_MODEL), const),
                  pl.BlockSpec((1, D_MODEL), const),
                  pl.BlockSpec((1, D_MODEL), const)],
        out_specs=[pl.BlockSpec((CONV_TILE, D_MODEL), lambda b, j: (b * nt + j, 0)),
                   pl.BlockSpec((1, CONV_HALO, D_MODEL), lambda b, j: (b, 0, 0))],
        out_shape=[jax.ShapeDtypeStruct((T_PROMPT, D_MODEL), BF16),
                   jax.ShapeDtypeStruct((BATCH, CONV_HALO, D_MODEL), F32)],
        scratch_shapes=[pltpu.VMEM((CONV_HALO + CONV_TILE, D_MODEL), F32)],
        compiler_params=_params("parallel", "arbitrary"),
        name="conv_prompt",
    )(x_all, x_all, g, w1, b1, wdw, bdw, lng, lnb)


SAMPLE_EXT = 40


def _conv_sample_kernel(x_ref, st_ref, g_ref, w1_ref, b1_ref, wsh_ref, bdw_ref, lng_ref, lnb_ref,
                        act_ref, u_ref):
    u = _glu(_rms(x_ref[...], g_ref[...]).astype(BF16), w1_ref, b1_ref)
    u_ref[...] = u
    pad = jnp.zeros((SAMPLE_EXT - HIST - DEC_SEQ, D_MODEL), F32)
    rows = []
    for b in range(SAMPLE_BATCH_BLOCK):
        ext = jnp.concatenate([st_ref[b], u[b * DEC_SEQ:(b + 1) * DEC_SEQ, :], pad], axis=0)
        for t in range(DEC_SEQ):
            rows.append(jnp.sum(wsh_ref[t] * ext, axis=0, keepdims=True))
    c = jnp.concatenate(rows, axis=0) + bdw_ref[...]
    act_ref[...] = _ln_swish(c, lng_ref[...], lnb_ref[...]).astype(BF16)


def _conv_sample(x_all, state, g, w1, b1, wsh, bdw, lng, lnb):
    bb = SAMPLE_BATCH_BLOCK
    rows = bb * DEC_SEQ
    first = T_PROMPT // rows
    const = lambda i: (0, 0)
    return pl.pallas_call(
        _conv_sample_kernel,
        grid=(DEC_BATCH // bb,),
        in_specs=[pl.BlockSpec((rows, D_MODEL), lambda i: (first + i, 0)),
                  pl.BlockSpec((bb, HIST, D_MODEL), lambda i: (i, 0, 0)),
                  pl.BlockSpec((1, D_MODEL), const),
                  pl.BlockSpec((D_MODEL, 2 * D_MODEL), const),
                  pl.BlockSpec((1, 2 * D_MODEL), const),
                  pl.BlockSpec((DEC_SEQ, SAMPLE_EXT, D_MODEL), lambda i: (0, 0, 0)),
                  pl.BlockSpec((1, D_MODEL), const),
                  pl.BlockSpec((1, D_MODEL), const),
                  pl.BlockSpec((1, D_MODEL), const)],
        out_specs=[pl.BlockSpec((rows, D_MODEL), lambda i: (i, 0)),
                   pl.BlockSpec((rows, D_MODEL), lambda i: (i, 0))],
        out_shape=[jax.ShapeDtypeStruct((T_SAMPLE, D_MODEL), BF16),
                   jax.ShapeDtypeStruct((T_SAMPLE, D_MODEL), F32)],
        compiler_params=_params("parallel"),
        name="conv_sample",
    )(x_all, state, g, w1, b1, wsh, bdw, lng, lnb)


def _router_weights(w_gr, b_gr, w_er, b_er):
    w = jnp.zeros((D_MODEL, ROUTER_COLS), F32)
    w = w.at[:, :N_EXPERTS].set(w_er).at[:, N_EXPERTS:N_EXPERTS + N_GROUPS].set(w_gr)
    b = jnp.zeros((1, ROUTER_COLS), F32)
    b = b.at[0, :N_EXPERTS].set(b_er).at[0, N_EXPERTS:N_EXPERTS + N_GROUPS].set(b_gr)
    return w, b


def _shifted_taps(w_dw):
    frames = [jnp.pad(w_dw, ((t, SAMPLE_EXT - CONV_WIDTH - t), (0, 0))) for t in range(DEC_SEQ)]
    return jnp.stack(frames)


_NB = SEQ // BLOCK
_SAMPLE_BLOCKS = [T_PROMPT // BLOCK + i for i in range(T_SAMPLE // BLOCK)]
PRECISE_KV_BLOCKS = [b * _NB + i for b in range(BATCH) for i in (_NB - 2, _NB - 1)] + _SAMPLE_BLOCKS
PRECISE_Q_BLOCKS = [b * _NB + _NB - 1 for b in range(BATCH)] + _SAMPLE_BLOCKS
_PER_TILE = MOE_TILE // BLOCK
PATCH_SRC = [(b if s == _PER_TILE - 1 else -1) for b in range(BATCH) for s in range(_PER_TILE)] + list(
    range(BATCH, BATCH + len(_SAMPLE_BLOCKS)))
PATCH_TOK = [PRECISE_Q_BLOCKS[max(s, 0)] for s in PATCH_SRC]
assert _NB >= 2 and WIN_BUF == BLOCK and T_SAMPLE % MOE_TILE == 0 and SEQ % MOE_TILE == 0


def kernel(x_prompt, x_sample, cache_k, cache_v, state_conv, w_qkv, w_o, attn_sinks, w_pw1, b_pw1, w_dw, b_dw, conv_ln_g, conv_ln_b, w_pw2, b_pw2, norm_mix_g, norm_ffn_g, w_group_router, b_group_router, w_expert_router, b_expert_router, w_gate, w_up, w_down, final_norm_g):
    row = lambda v: v.reshape(1, -1)
    x = jnp.concatenate([x_prompt.reshape(T_PROMPT, D_MODEL), x_sample.reshape(T_SAMPLE, D_MODEL)], axis=0)
    outs = {}
    for layer in range(DEPTH):
        j = layer // 2
        g_ffn = row(norm_ffn_g[layer])
        wr, br = _router_weights(w_group_router[layer], b_group_router[layer],
                                 w_expert_router[layer], b_expert_router[layer])
        if layer % 2 == 0:
            g_mix = row(norm_mix_g[layer])
            ck = cache_k[j].reshape(DEC_BATCH, WIN_BUF, KV_DIM)
            cv = cache_v[j].reshape(DEC_BATCH, WIN_BUF, KV_DIM)
            q, kv = _qkv(x, g_mix, w_qkv[j].astype(BF16))
            o_p = _attn_prompt(attn_sinks[j], q, kv)
            o_s = _attn_sample(attn_sinks[j], q, kv, ck, cv, T_PROMPT, False)
            mixed = jnp.concatenate([o_p, o_s], axis=0)
            b_out = jnp.zeros((1, D_MODEL), F32)

            q_c, kv_c = _qkv_precise(jnp.asarray(PRECISE_KV_BLOCKS, jnp.int32), x, g_mix, w_qkv[j])
            n_prompt_rows = 2 * BATCH * BLOCK
            o_cp = _attn_prompt_last_precise(attn_sinks[j], q_c, kv_c)
            o_cs = _attn_sample(attn_sinks[j], q_c, kv_c, ck, cv, n_prompt_rows, True)
            patch = _post_precise(jnp.asarray(PATCH_SRC, jnp.int32), jnp.asarray(PATCH_TOK, jnp.int32),
                                  jnp.concatenate([o_cp, o_cs], axis=0), w_o[j], b_out, x, g_ffn, wr, br)
            x1, xs, meta, counts = _post(mixed, w_o[j].astype(BF16), b_out, x, g_ffn, wr.astype(BF16), br, patch)

            kv_p = kv_c[:n_prompt_rows].reshape(BATCH, 2, BLOCK, 2 * KV_DIM)[:, 1]
            kv_s = kv_c[n_prompt_rows:].reshape(DEC_BATCH, DEC_SEQ, 2 * KV_DIM)
            shape5 = lambda a: a.reshape(a.shape[0], WIN_BUF, N_KV_HEADS, HEAD_DIM)
            outs.setdefault("k_p", []).append(shape5(kv_p[..., :KV_DIM]))
            outs.setdefault("v_p", []).append(shape5(kv_p[..., KV_DIM:]))
            outs.setdefault("k_s", []).append(shape5(jnp.concatenate([ck[:, DEC_SEQ:], kv_s[..., :KV_DIM]], axis=1)))
            outs.setdefault("v_s", []).append(shape5(jnp.concatenate([cv[:, DEC_SEQ:], kv_s[..., KV_DIM:]], axis=1)))
        else:
            g = row(norm_mix_g[layer])
            w1, b1 = w_pw1[j].astype(BF16), row(b_pw1[j])
            bdw, lng, lnb = row(b_dw[j]), row(conv_ln_g[j]), row(conv_ln_b[j])
            act_p, tail = _conv_prompt(x, g, w1, b1, w_dw[j], bdw, lng, lnb)
            act_s, u_s = _conv_sample(x, state_conv[j], g, w1, b1, _shifted_taps(w_dw[j]), bdw, lng, lnb)
            mixed = jnp.concatenate([act_p, act_s], axis=0)
            x1, xs, meta, counts = _post(mixed, w_pw2[j].astype(BF16), row(b_pw2[j]), x, g_ffn,
                                         wr.astype(BF16), br)
            outs.setdefault("c_p", []).append(tail[:, CONV_HALO - HIST:])
            u_s = u_s.reshape(DEC_BATCH, DEC_SEQ, D_MODEL)
            outs.setdefault("c_s", []).append(jnp.concatenate([state_conv[j][:, DEC_SEQ:], u_s], axis=1))
        x = _moe(xs, meta, counts, x1, w_gate[layer], w_up[layer], w_down[layer], row(final_norm_g),
                 final=layer == DEPTH - 1)
    y = x
    y_prompt = y[:T_PROMPT].reshape(BATCH, SEQ, D_MODEL)
    y_sample = y[T_PROMPT:].reshape(DEC_BATCH, DEC_SEQ, D_MODEL)
    return (y_prompt, y_sample, jnp.stack(outs["k_p"]), jnp.stack(outs["v_p"]), jnp.stack(outs["c_p"]),
            jnp.stack(outs["k_s"]), jnp.stack(outs["v_s"]), jnp.stack(outs["c_s"]))
```

```python
import functools

import jax
import jax.numpy as jnp
from jax import lax
from jax.experimental import pallas as pl
from jax.experimental.pallas import tpu as pltpu

D_MODEL = 1024
BATCH = 4
SEQ = 4096
DEPTH = 2
DEC_BATCH = 128
DEC_SEQ = 4
HEAD_DIM = 64
N_HEADS = 16
N_KV_HEADS = 4
GQA_GROUP = 4
WINDOW = 128
BLOCK = 128
WIN_BUF = 128
CONV_WIDTH = 31
HIST = CONV_WIDTH - 1
N_GROUPS = 4
EXPERTS_PER_GROUP = 8
N_EXPERTS = 32
D_EXPERT = 256
EPS = 1e-6
NEG_INF = -1e30

T_PROMPT = BATCH * SEQ
T_SAMPLE = DEC_BATCH * DEC_SEQ
T_ALL = T_PROMPT + T_SAMPLE
KV_DIM = N_KV_HEADS * HEAD_DIM
QKV_OUT = D_MODEL + 2 * KV_DIM

LANES = 128
SUBLANES = 8
TOKEN_TILE = 512
CONV_TILE = 512
CONV_ROWS = 16
CONV_HALO = 32
SAMPLE_BATCH_BLOCK = 8
VMEM_LIMIT = 48 * 1024 * 1024

F32 = jnp.float32
BF16 = jnp.bfloat16


def _params(*sem):
    return pltpu.CompilerParams(dimension_semantics=sem, vmem_limit_bytes=VMEM_LIMIT)


def _rms(x, g):
    return x * lax.rsqrt(jnp.mean(x * x, axis=-1, keepdims=True) + EPS) * g


def _mm(a, b, precise, contract_b=0):
    dims = (((1,), (contract_b,)), ((), ()))
    if precise:
        return lax.dot_general(a.astype(F32), b.astype(F32), dims, precision=lax.Precision.HIGHEST,
                               preferred_element_type=F32)
    return lax.dot_general(a.astype(BF16), b.astype(BF16), dims, preferred_element_type=F32)


def _qkv_kernel(x_ref, g_ref, w_ref, q_ref, kv_ref, *, precise):
    qkv = _mm(_rms(x_ref[...], g_ref[...]), w_ref[...], precise)
    q_ref[...] = (qkv[:, :D_MODEL] * (HEAD_DIM ** -0.5)).astype(q_ref.dtype)
    kv_ref[...] = qkv[:, D_MODEL:]


def _qkv_precise_kernel(tbl_ref, x_ref, g_ref, w_ref, q_ref, kv_ref):
    del tbl_ref
    _qkv_kernel(x_ref, g_ref, w_ref, q_ref, kv_ref, precise=True)


def _qkv_precise(blocks, x, g, w):
    n = blocks.shape[0]
    return pl.pallas_call(
        _qkv_precise_kernel,
        grid_spec=pltpu.PrefetchScalarGridSpec(
            num_scalar_prefetch=1,
            grid=(n,),
            in_specs=[pl.BlockSpec((BLOCK, D_MODEL), lambda i, t: (t[i], 0)),
                      pl.BlockSpec((1, D_MODEL), lambda i, t: (0, 0)),
                      pl.BlockSpec((D_MODEL, QKV_OUT), lambda i, t: (0, 0))],
            out_specs=[pl.BlockSpec((BLOCK, D_MODEL), lambda i, t: (i, 0)),
                       pl.BlockSpec((BLOCK, 2 * KV_DIM), lambda i, t: (i, 0))],
        ),
        out_shape=[jax.ShapeDtypeStruct((n * BLOCK, D_MODEL), F32),
                   jax.ShapeDtypeStruct((n * BLOCK, 2 * KV_DIM), F32)],
        compiler_params=_params("arbitrary"),
        name="qkv_proj_precise",
    )(blocks, x, g, w)


def _qkv(x, g, w):
    n = x.shape[0] // TOKEN_TILE
    return pl.pallas_call(
        functools.partial(_qkv_kernel, precise=False),
        grid=(n,),
        in_specs=[pl.BlockSpec((TOKEN_TILE, D_MODEL), lambda i: (i, 0)),
                  pl.BlockSpec((1, D_MODEL), lambda i: (0, 0)),
                  pl.BlockSpec((D_MODEL, QKV_OUT), lambda i: (0, 0))],
        out_specs=[pl.BlockSpec((TOKEN_TILE, D_MODEL), lambda i: (i, 0)),
                   pl.BlockSpec((TOKEN_TILE, 2 * KV_DIM), lambda i: (i, 0))],
        out_shape=[jax.ShapeDtypeStruct((x.shape[0], D_MODEL), BF16),
                   jax.ShapeDtypeStruct((x.shape[0], 2 * KV_DIM), F32)],
        compiler_params=_params("parallel"),
        name="qkv_proj",
    )(x, g, w)


def _alibi_slope(head):
    return 2.0 ** (-8.0 * (head + 1) / N_HEADS)


def _softmax_pv(s, dist_f, valid, slope, sink, v, precise):
    logits = jnp.where(valid, s - slope * dist_f, NEG_INF)
    m = jnp.maximum(jnp.max(logits, axis=-1, keepdims=True), sink)
    p = jnp.exp(logits - m)
    denom = jnp.sum(p, axis=-1, keepdims=True) + jnp.exp(sink - m)
    return _mm(p, v, precise) / denom


def _attn_prompt_kernel(sink_ref, q_ref, kvp_ref, kvc_ref, o_ref, *, precise, first_block):
    q = q_ref[...]
    kv = jnp.concatenate([kvp_ref[...], kvc_ref[...]], axis=0)
    kv = kv if precise else kv.astype(BF16)
    qi = lax.broadcasted_iota(jnp.int32, (BLOCK, 2 * BLOCK), 0)
    kj = lax.broadcasted_iota(jnp.int32, (BLOCK, 2 * BLOCK), 1)
    dist = qi + BLOCK - kj
    valid = (dist >= 0) & (dist < WINDOW) & ((kj >= BLOCK) | jnp.logical_not(first_block()))
    dist_f = dist.astype(F32)
    for kvh in range(N_KV_HEADS):
        k_h = kv[:, kvh * HEAD_DIM:(kvh + 1) * HEAD_DIM]
        v_h = kv[:, KV_DIM + kvh * HEAD_DIM:KV_DIM + (kvh + 1) * HEAD_DIM]
        for g in range(GQA_GROUP):
            head = kvh * GQA_GROUP + g
            q_h = q[:, head * HEAD_DIM:(head + 1) * HEAD_DIM]
            s = _mm(q_h, k_h, precise, contract_b=1)
            o = _softmax_pv(s, dist_f, valid, _alibi_slope(head), sink_ref[head], v_h, precise)
            o_ref[:, head * HEAD_DIM:(head + 1) * HEAD_DIM] = o.astype(o_ref.dtype)


def _attn_prompt_last_precise(sinks, q_c, kv_c):
    return pl.pallas_call(
        functools.partial(_attn_prompt_kernel, precise=True, first_block=lambda: SEQ // BLOCK == 1),
        grid_spec=pltpu.PrefetchScalarGridSpec(
            num_scalar_prefetch=1,
            grid=(BATCH,),
            in_specs=[pl.BlockSpec((BLOCK, D_MODEL), lambda b, s: (2 * b + 1, 0)),
                      pl.BlockSpec((BLOCK, 2 * KV_DIM), lambda b, s: (2 * b, 0)),
                      pl.BlockSpec((BLOCK, 2 * KV_DIM), lambda b, s: (2 * b + 1, 0))],
            out_specs=pl.BlockSpec((BLOCK, D_MODEL), lambda b, s: (b, 0)),
        ),
        out_shape=jax.ShapeDtypeStruct((BATCH * BLOCK, D_MODEL), F32),
        compiler_params=_params("arbitrary"),
        name="attn_prompt_precise",
    )(sinks, q_c, kv_c, kv_c)


def _attn_prompt(sinks, q_all, kv_all):
    nb = SEQ // BLOCK
    return pl.pallas_call(
        functools.partial(_attn_prompt_kernel, precise=False, first_block=lambda: pl.program_id(1) == 0),
        grid_spec=pltpu.PrefetchScalarGridSpec(
            num_scalar_prefetch=1,
            grid=(BATCH, nb),
            in_specs=[pl.BlockSpec((BLOCK, D_MODEL), lambda b, i, s: (b * nb + i, 0)),
                      pl.BlockSpec((BLOCK, 2 * KV_DIM), lambda b, i, s: (b * nb + jnp.maximum(i - 1, 0), 0)),
                      pl.BlockSpec((BLOCK, 2 * KV_DIM), lambda b, i, s: (b * nb + i, 0))],
            out_specs=pl.BlockSpec((BLOCK, D_MODEL), lambda b, i, s: (b * nb + i, 0)),
        ),
        out_shape=jax.ShapeDtypeStruct((T_PROMPT, D_MODEL), BF16),
        compiler_params=_params("parallel", "parallel"),
        name="attn_prompt",
    )(sinks, q_all, kv_all, kv_all)


SAMPLE_KEYS = WIN_BUF + 16


def _attn_sample_kernel(sink_ref, q_ref, kvn_ref, ck_ref, cv_ref, o_ref, *, precise):
    rows = GQA_GROUP * DEC_SEQ
    t_q = lax.broadcasted_iota(jnp.int32, (rows, SAMPLE_KEYS), 0) % DEC_SEQ
    s_k = lax.broadcasted_iota(jnp.int32, (rows, SAMPLE_KEYS), 1)
    dist = WIN_BUF + t_q - s_k
    valid = (dist >= 0) & (dist < WINDOW) & (s_k < WIN_BUF + DEC_SEQ)
    dist_f = dist.astype(F32)
    g_row = lax.broadcasted_iota(jnp.int32, (rows, 1), 0) // DEC_SEQ
    pad = jnp.zeros((SAMPLE_KEYS - WIN_BUF - DEC_SEQ, KV_DIM), F32)
    for b in range(SAMPLE_BATCH_BLOCK):
        r0 = b * DEC_SEQ
        q_b = q_ref[r0:r0 + DEC_SEQ, :]
        kvn = kvn_ref[r0:r0 + DEC_SEQ, :]
        k_all = jnp.concatenate([ck_ref[b], kvn[:, :KV_DIM], pad], axis=0)
        v_all = jnp.concatenate([cv_ref[b], kvn[:, KV_DIM:], pad], axis=0)
        if not precise:
            k_all, v_all = k_all.astype(BF16), v_all.astype(BF16)
        for kvh in range(N_KV_HEADS):
            k_h = k_all[:, kvh * HEAD_DIM:(kvh + 1) * HEAD_DIM]
            v_h = v_all[:, kvh * HEAD_DIM:(kvh + 1) * HEAD_DIM]
            q_h = jnp.concatenate(
                [q_b[:, (kvh * GQA_GROUP + g) * HEAD_DIM:(kvh * GQA_GROUP + g + 1) * HEAD_DIM]
                 for g in range(GQA_GROUP)], axis=0)
            s = _mm(q_h, k_h, precise, contract_b=1)
            slope = jnp.zeros((rows, 1), F32)
            sink = jnp.zeros((rows, 1), F32)
            for g in range(GQA_GROUP):
                head = kvh * GQA_GROUP + g
                slope = jnp.where(g_row == g, _alibi_slope(head), slope)
                sink = jnp.where(g_row == g, sink_ref[head], sink)
            o = _softmax_pv(s, dist_f, valid, slope, sink, v_h, precise).astype(o_ref.dtype)
            for g in range(GQA_GROUP):
                head = kvh * GQA_GROUP + g
                o_ref[r0:r0 + DEC_SEQ, head * HEAD_DIM:(head + 1) * HEAD_DIM] = (
                    o[g * DEC_SEQ:(g + 1) * DEC_SEQ, :])


def _attn_sample(sinks, q_all, kv_all, cache_k, cache_v, first_row, precise):
    bb = SAMPLE_BATCH_BLOCK
    rows = bb * DEC_SEQ
    first = first_row // rows
    return pl.pallas_call(
        functools.partial(_attn_sample_kernel, precise=precise),
        grid_spec=pltpu.PrefetchScalarGridSpec(
            num_scalar_prefetch=1,
            grid=(DEC_BATCH // bb,),
            in_specs=[pl.BlockSpec((rows, D_MODEL), lambda i, s: (first + i, 0)),
                      pl.BlockSpec((rows, 2 * KV_DIM), lambda i, s: (first + i, 0)),
                      pl.BlockSpec((bb, WIN_BUF, KV_DIM), lambda i, s: (i, 0, 0)),
                      pl.BlockSpec((bb, WIN_BUF, KV_DIM), lambda i, s: (i, 0, 0))],
            out_specs=pl.BlockSpec((rows, D_MODEL), lambda i, s: (i, 0)),
        ),
        out_shape=jax.ShapeDtypeStruct((T_SAMPLE, D_MODEL), F32 if precise else BF16),
        compiler_params=_params("parallel"),
        name="attn_sample_precise" if precise else "attn_sample",
    )(sinks, q_all, kv_all, cache_k, cache_v)


ROUTER_COLS = LANES


def _route(logits):
    lane = lax.broadcasted_iota(jnp.int32, logits.shape, 1)
    is_group = (lane >= N_EXPERTS) & (lane < N_EXPERTS + N_GROUPS)
    gl = jnp.where(is_group, logits, -jnp.inf)
    g_max = jnp.max(gl, axis=-1, keepdims=True)
    g_idx = jnp.min(jnp.where(gl == g_max, lane - N_EXPERTS, N_GROUPS), axis=-1, keepdims=True)
    g_w = 1.0 / jnp.sum(jnp.exp(gl - g_max), axis=-1, keepdims=True)
    in_group = (lane < N_EXPERTS) & ((lane // EXPERTS_PER_GROUP) == g_idx)
    el = jnp.where(in_group, logits, -jnp.inf)
    m1 = jnp.max(el, axis=-1, keepdims=True)
    i1 = jnp.min(jnp.where(el == m1, lane, ROUTER_COLS), axis=-1, keepdims=True)
    el2 = jnp.where(lane == i1, -jnp.inf, el)
    m2 = jnp.max(el2, axis=-1, keepdims=True)
    i2 = jnp.min(jnp.where(el2 == m2, lane, ROUTER_COLS), axis=-1, keepdims=True)
    r = jnp.exp(m2 - m1)
    w1 = g_w / (1.0 + r)
    w2 = g_w * r / (1.0 + r)
    return i1, i2, w1, w2


MOE_TILE = TOKEN_TILE
ROW_ALIGN = 16
SLOTS = 2 * MOE_TILE + N_EXPERTS * ROW_ALIGN
X_COLS = D_MODEL + LANES
N_TILES = T_ALL // MOE_TILE
ITEM_ROWS = 1024
SUB_ROWS = 256
MAX_ITEMS = N_TILES * SLOTS // ITEM_ROWS + N_EXPERTS


def _split3(w):
    hi = w.astype(BF16)
    r = w - hi.astype(F32)
    mid = r.astype(BF16)
    lo = (r - mid.astype(F32)).astype(BF16)
    return hi.astype(F32), mid.astype(F32), lo.astype(F32)


def _lane_pack(lane, cols):
    out = jnp.zeros(lane.shape, F32)
    for k, c in enumerate(cols):
        out = jnp.where(lane == k, c, out)
    return out


def _post_kernel(*refs, patched):
    if patched:
        (a_ref, w_ref, b_ref, x_ref, g_ref, wr_ref, br_ref, px1_ref, plg_ref,
         x1_ref, xs_ref, meta_ref, cnt_ref) = refs
    else:
        a_ref, w_ref, b_ref, x_ref, g_ref, wr_ref, br_ref, x1_ref, xs_ref, meta_ref, cnt_ref = refs
    tile = pl.program_id(0)
    x1 = x_ref[...] + (_mm(a_ref[...], w_ref[...], False) + b_ref[...])
    if patched:
        row = lax.broadcasted_iota(jnp.int32, (MOE_TILE, 1), 0)
        tiles_per_seq = SEQ // MOE_TILE
        use = (tile >= T_PROMPT // MOE_TILE) | (
            (tile % tiles_per_seq == tiles_per_seq - 1) & (row >= MOE_TILE - BLOCK))
        x1 = jnp.where(use, px1_ref[...], x1)
    x1_ref[...] = x1
    h = _rms(x1, g_ref[...])
    logits = _mm(h, wr_ref[...], False) + br_ref[...]
    if patched:
        logits = jnp.where(use, plg_ref[...], logits)
    i1, i2, w1, w2 = _route(logits)

    lane = lax.broadcasted_iota(jnp.int32, (MOE_TILE, LANES), 1)
    a1, a2 = lane == i1, lane == i2
    assigned = jnp.where(a1 | a2, 1.0, 0.0).astype(BF16)
    r_t = lax.broadcasted_iota(jnp.int32, (MOE_TILE, MOE_TILE), 0)
    c_t = lax.broadcasted_iota(jnp.int32, (MOE_TILE, MOE_TILE), 1)
    before = jnp.where(c_t < r_t, 1.0, 0.0).astype(BF16)
    rank = jnp.dot(before, assigned, preferred_element_type=F32)
    count = rank[MOE_TILE - 1:, :] + assigned[MOE_TILE - 1:, :].astype(F32)
    padded = jnp.floor((count + (ROW_ALIGN - 1)) * (1.0 / ROW_ALIGN)) * ROW_ALIGN
    r_e = lax.broadcasted_iota(jnp.int32, (LANES, LANES), 0)
    c_e = lax.broadcasted_iota(jnp.int32, (LANES, LANES), 1)
    lower_experts = jnp.where(r_e < c_e, 1.0, 0.0).astype(BF16)
    start = jnp.dot(jnp.broadcast_to(padded, (8, LANES)).astype(BF16), lower_experts,
                    preferred_element_type=F32)[0:1, :]
    pos = start + rank
    s1 = jnp.sum(jnp.where(a1, pos, 0.0), axis=-1, keepdims=True)
    s2 = jnp.sum(jnp.where(a2, pos, 0.0), axis=-1, keepdims=True)
    meta = _lane_pack(lane, [s1, s2])
    meta_ref[...] = meta
    cnt_ref[0] = jnp.broadcast_to(padded, (8, LANES)).astype(jnp.int32)

    slots_t = meta.T
    s_iota = lax.broadcasted_iota(jnp.int32, (SLOTS, MOE_TILE), 0)
    p1 = jnp.where(s_iota == slots_t[0:1, :].astype(jnp.int32), 1.0, 0.0).astype(BF16)
    p2 = jnp.where(s_iota == slots_t[1:2, :].astype(jnp.int32), 1.0, 0.0).astype(BF16)
    xs_ref[:, :D_MODEL] = jnp.dot(p1 + p2, h.astype(BF16), preferred_element_type=F32).astype(BF16)
    wcols = (jnp.dot(p1, _lane_pack(lane, _split3(w1)).astype(BF16), preferred_element_type=F32)
             + jnp.dot(p2, _lane_pack(lane, _split3(w2)).astype(BF16), preferred_element_type=F32))
    xs_ref[:, D_MODEL:] = wcols.astype(BF16)


def _post(a, w, b, x, g, wr, br, patch=None):
    tile = lambda i: (i, 0)
    const = lambda i: (0, 0)
    in_specs = [pl.BlockSpec((MOE_TILE, D_MODEL), tile),
                pl.BlockSpec((D_MODEL, D_MODEL), const),
                pl.BlockSpec((1, D_MODEL), const),
                pl.BlockSpec((MOE_TILE, D_MODEL), tile),
                pl.BlockSpec((1, D_MODEL), const),
                pl.BlockSpec((D_MODEL, ROUTER_COLS), const),
                pl.BlockSpec((1, ROUTER_COLS), const)]
    args = [a, w, b, x, g, wr, br]
    if patch is not None:
        tiles_per_seq = SEQ // MOE_TILE
        pidx = lambda i: (jnp.where(i >= BATCH * tiles_per_seq, i - BATCH * (tiles_per_seq - 1),
                                    i // tiles_per_seq), 0)
        in_specs += [pl.BlockSpec((MOE_TILE, D_MODEL), pidx), pl.BlockSpec((MOE_TILE, ROUTER_COLS), pidx)]
        args += list(patch)
    return pl.pallas_call(
        functools.partial(_post_kernel, patched=patch is not None),
        grid=(N_TILES,),
        in_specs=in_specs,
        out_specs=[pl.BlockSpec((MOE_TILE, D_MODEL), tile),
                   pl.BlockSpec((SLOTS, X_COLS), tile),
                   pl.BlockSpec((MOE_TILE, LANES), tile),
                   pl.BlockSpec((1, 8, LANES), lambda i: (i, 0, 0))],
        out_shape=[jax.ShapeDtypeStruct(x.shape, F32),
                   jax.ShapeDtypeStruct((N_TILES * SLOTS, X_COLS), BF16),
                   jax.ShapeDtypeStruct((x.shape[0], LANES), F32),
                   jax.ShapeDtypeStruct((N_TILES, 8, LANES), jnp.int32)],
        compiler_params=_params("parallel"),
        name="mixer_out_route_sort",
    )(*args)


def _work_plan(counts):
    i32 = lambda v: v.astype(jnp.int32)
    tile_start = jnp.cumsum(counts, axis=1) - counts
    tile_rows = jnp.sum(counts, axis=1)
    seg_rows = jnp.sum(counts, axis=0)
    offset = (jnp.cumsum(counts, axis=0) - counts).T
    n_items_e = (seg_rows + ITEM_ROWS - 1) // ITEM_ROWS
    item_end = jnp.cumsum(n_items_e)
    m = jnp.arange(MAX_ITEMS, dtype=jnp.int32)
    expert = jnp.minimum(jnp.sum(m[:, None] >= item_end[None, :], axis=1), N_EXPERTS - 1)
    base = (m - (item_end - n_items_e)[expert]) * ITEM_ROWS
    off_m = offset[expert]
    lo = jnp.sum(off_m + counts.T[expert] <= base[:, None], axis=1)
    hi = jnp.sum(off_m < (base + ITEM_ROWS)[:, None], axis=1)
    rows = jnp.clip(seg_rows[expert] - base, 0, ITEM_ROWS)
    chunk_row = (jnp.arange(N_TILES, dtype=jnp.int32)[:, None] * SLOTS + tile_start).T
    return (i32(expert), i32(lo), i32(hi), i32(base), i32(rows), i32(item_end[-1]).reshape(1),
            i32(chunk_row.reshape(-1)), i32(counts.T.reshape(-1)), i32(offset.reshape(-1)),
            i32(jnp.arange(N_TILES) * SLOTS + tile_rows), i32(SLOTS - tile_rows))


def _expert_kernel(ie_ref, lo_ref, hi_ref, base_ref, rows_ref, n_ref, crow_ref, clen_ref, coff_ref,
                   tdst_ref, tlen_ref, xs_hbm, wg_ref, wu_ref, wd_ref, y_hbm,
                   xbuf, ybuf, wgu_scr, wd_scr, zbuf, gsem, ssem, zsem):
    k = pl.program_id(0)
    n = n_ref[0]

    def for_chunks(m, fn):
        e, base = ie_ref[m], base_ref[m]
        end = base + rows_ref[m]

        def body(i, carry):
            c = e * N_TILES + i
            first = jnp.maximum(coff_ref[c], base)
            rows = pl.multiple_of(jnp.minimum(coff_ref[c] + clen_ref[c], end) - first, ROW_ALIGN)

            @pl.when(rows > 0)
            def _():
                fn(pl.multiple_of(crow_ref[c] + first - coff_ref[c], ROW_ALIGN),
                   pl.multiple_of(first - base, ROW_ALIGN), rows)
            return carry

        lax.fori_loop(lo_ref[m], hi_ref[m], body, 0)

    def gather(m):
        slot = m % 2
        for_chunks(m, lambda src, dst, rows: pltpu.make_async_copy(
            xs_hbm.at[pl.ds(src, rows)], xbuf.at[slot, pl.ds(dst, rows)], gsem.at[slot]).start())

    def scatter(m):
        slot = m % 2
        for_chunks(m, lambda dst, src, rows: pltpu.make_async_copy(
            ybuf.at[slot, pl.ds(src, rows)], y_hbm.at[pl.ds(dst, rows)], ssem.at[slot]).start())

    def wait_item(buf, sem, m):
        slot = m % 2
        rows = pl.multiple_of(rows_ref[m], ROW_ALIGN)

        @pl.when(rows > 0)
        def _():
            pltpu.make_async_copy(buf.at[slot, pl.ds(0, rows)], buf.at[slot, pl.ds(0, rows)], sem.at[slot]).wait()

    def tails(fn):
        for t in range(N_TILES):
            rows = pl.multiple_of(tlen_ref[t], ROW_ALIGN)

            @pl.when(rows > 0)
            def _():
                fn(pltpu.make_async_copy(zbuf.at[pl.ds(0, rows)],
                                         y_hbm.at[pl.ds(pl.multiple_of(tdst_ref[t], ROW_ALIGN), rows)], zsem))

    @pl.when(k == 0)
    def _():
        xbuf[...] = jnp.zeros_like(xbuf)
        zbuf[...] = jnp.zeros_like(zbuf)
        tails(lambda cp: cp.start())

        @pl.when(n > 0)
        def _():
            gather(0)

    @pl.when(k < n)
    def _():
        @pl.when(k + 1 < n)
        def _():
            gather(k + 1)

        @pl.when((k == 0) | (ie_ref[k] != ie_ref[jnp.maximum(k - 1, 0)]))
        def _():
            wgu_scr[:, :D_EXPERT] = wg_ref[0].astype(BF16)
            wgu_scr[:, D_EXPERT:] = wu_ref[0].astype(BF16)
            wd_scr[...] = wd_ref[0].astype(BF16)

        slot = k % 2
        wait_item(xbuf, gsem, k)

        @pl.when(k >= 2)
        def _():
            wait_item(ybuf, ssem, k - 2)

        for sb in range(ITEM_ROWS // SUB_ROWS):
            @pl.when(sb * SUB_ROWS < rows_ref[k])
            def _():
                r0 = sb * SUB_ROWS
                x = xbuf[slot, r0:r0 + SUB_ROWS, :]
                gu = jnp.dot(x[:, :D_MODEL], wgu_scr[...], preferred_element_type=F32)
                gate, up = gu[:, :D_EXPERT], gu[:, D_EXPERT:]
                hid = (gate * jax.nn.sigmoid(gate) * up).astype(BF16)
                y = jnp.dot(hid, wd_scr[...], preferred_element_type=F32)
                wparts = x[:, D_MODEL:].astype(F32)
                weight = wparts[:, 0:1] + wparts[:, 1:2] + wparts[:, 2:3]
                ybuf[slot, r0:r0 + SUB_ROWS, :] = (weight * y).astype(BF16)

        scatter(k)

        @pl.when(k == n - 1)
        def _():
            @pl.when(k >= 1)
            def _():
                wait_item(ybuf, ssem, k - 1)
            wait_item(ybuf, ssem, k)
            tails(lambda cp: cp.wait())


def _experts(plan, xs, wg, wu, wd):
    wsel = lambda k, *p: (p[0][jnp.minimum(k, jnp.maximum(p[5][0] - 1, 0))], 0, 0)
    any_space = pl.BlockSpec(memory_space=pl.ANY)
    return pl.pallas_call(
        _expert_kernel,
        grid_spec=pltpu.PrefetchScalarGridSpec(
            num_scalar_prefetch=len(plan),
            grid=(MAX_ITEMS,),
            in_specs=[any_space,
                      pl.BlockSpec((1, D_MODEL, D_EXPERT), wsel),
                      pl.BlockSpec((1, D_MODEL, D_EXPERT), wsel),
                      pl.BlockSpec((1, D_EXPERT, D_MODEL), wsel)],
            out_specs=any_space,
            scratch_shapes=[pltpu.VMEM((2, ITEM_ROWS, X_COLS), BF16),
                            pltpu.VMEM((2, ITEM_ROWS, D_MODEL), BF16),
                            pltpu.VMEM((D_MODEL, 2 * D_EXPERT), BF16),
                            pltpu.VMEM((D_EXPERT, D_MODEL), BF16),
                            pltpu.VMEM((MOE_TILE, D_MODEL), BF16),
                            pltpu.SemaphoreType.DMA((2,)),
                            pltpu.SemaphoreType.DMA((2,)),
                            pltpu.SemaphoreType.DMA(())],
        ),
        out_shape=jax.ShapeDtypeStruct((N_TILES * SLOTS, D_MODEL), BF16),
        compiler_params=_params("arbitrary"),
        name="experts",
    )(*plan, xs, wg, wu, wd)


def _combine_kernel(y_ref, meta_ref, x1_ref, g_ref, o_ref, *, final):
    meta = meta_ref[...]
    s_iota = lax.broadcasted_iota(jnp.int32, (MOE_TILE, SLOTS), 1)
    pick = ((s_iota == meta[:, 0:1].astype(jnp.int32)) | (s_iota == meta[:, 1:2].astype(jnp.int32)))
    moe = jnp.dot(jnp.where(pick, 1.0, 0.0).astype(BF16), y_ref[...], preferred_element_type=F32)
    x2 = x1_ref[...] + moe
    o_ref[...] = _rms(x2, g_ref[...]) if final else x2


def _combine(y_local, meta, x1, g_final, final):
    tile = lambda i: (i, 0)
    return pl.pallas_call(
        functools.partial(_combine_kernel, final=final),
        grid=(N_TILES,),
        in_specs=[pl.BlockSpec((SLOTS, D_MODEL), tile),
                  pl.BlockSpec((MOE_TILE, LANES), tile),
                  pl.BlockSpec((MOE_TILE, D_MODEL), tile),
                  pl.BlockSpec((1, D_MODEL), lambda i: (0, 0))],
        out_specs=pl.BlockSpec((MOE_TILE, D_MODEL), tile),
        out_shape=jax.ShapeDtypeStruct(x1.shape, F32),
        compiler_params=_params("parallel"),
        name="moe_combine",
    )(y_local, meta, x1, g_final)


def _moe(xs, meta, counts, x1, wg, wu, wd, g_final, final):
    plan = _work_plan(counts[:, 0, :N_EXPERTS])
    return _combine(_experts(plan, xs, wg, wu, wd), meta, x1, g_final, final)


def _post_precise_kernel(src_ref, tok_ref, a_ref, w_ref, b_ref, x_ref, g_ref, wr_ref, br_ref, x1_ref, lg_ref):
    del tok_ref
    filled = src_ref[pl.program_id(0)] >= 0

    @pl.when(filled)
    def _():
        x1 = x_ref[...] + (_mm(a_ref[...], w_ref[...], True) + b_ref[...])
        x1_ref[...] = x1
        lg_ref[...] = _mm(_rms(x1, g_ref[...]), wr_ref[...], True) + br_ref[...]

    @pl.when(jnp.logical_not(filled))
    def _():
        x1_ref[...] = jnp.zeros_like(x1_ref)
        lg_ref[...] = jnp.zeros_like(lg_ref)


def _post_precise(src_blocks, tok_blocks, a_c, w, b, x, g, wr, br):
    n = src_blocks.shape[0]
    const = lambda i, s, t: (0, 0)
    slot = lambda i, s, t: (i, 0)
    return pl.pallas_call(
        _post_precise_kernel,
        grid_spec=pltpu.PrefetchScalarGridSpec(
            num_scalar_prefetch=2,
            grid=(n,),
            in_specs=[pl.BlockSpec((BLOCK, D_MODEL), lambda i, s, t: (jnp.maximum(s[i], 0), 0)),
                      pl.BlockSpec((D_MODEL, D_MODEL), const),
                      pl.BlockSpec((1, D_MODEL), const),
                      pl.BlockSpec((BLOCK, D_MODEL), lambda i, s, t: (t[i], 0)),
                      pl.BlockSpec((1, D_MODEL), const),
                      pl.BlockSpec((D_MODEL, ROUTER_COLS), const),
                      pl.BlockSpec((1, ROUTER_COLS), const)],
            out_specs=[pl.BlockSpec((BLOCK, D_MODEL), slot),
                       pl.BlockSpec((BLOCK, ROUTER_COLS), slot)],
        ),
        out_shape=[jax.ShapeDtypeStruct((n * BLOCK, D_MODEL), F32),
                   jax.ShapeDtypeStruct((n * BLOCK, ROUTER_COLS), F32)],
        compiler_params=_params("arbitrary"),
        name="mixer_out_route_precise",
    )(src_blocks, tok_blocks, a_c, w, b, x, g, wr, br)


def _glu(h, w_ref, b_ref):
    a = jnp.dot(h, w_ref[...], preferred_element_type=F32) + b_ref[...]
    return a[:, :D_MODEL] * jax.nn.sigmoid(a[:, D_MODEL:])


def _ln_swish(c, g, b):
    cc = c - jnp.mean(c, axis=-1, keepdims=True)
    var = jnp.mean(cc * cc, axis=-1, keepdims=True)
    y = cc * lax.rsqrt(var + EPS) * g + b
    return y * jax.nn.sigmoid(y)


def _conv_prompt_kernel(halo_ref, x_ref, g_ref, w1_ref, b1_ref, wdw_ref, bdw_ref, lng_ref, lnb_ref,
                        act_ref, tail_ref, u_scr):
    j = pl.program_id(1)
    xh = jnp.concatenate([halo_ref[...], x_ref[...]], axis=0)
    u = _glu(_rms(xh, g_ref[...]).astype(BF16), w1_ref, b1_ref)
    n = CONV_HALO + CONV_TILE
    row = lax.broadcasted_iota(jnp.int32, (n, 1), 0)
    u_scr[0] = jnp.where((row < CONV_HALO) & (j == 0), 0.0, u)
    for r in range(1, SUBLANES):
        u_scr[r, 0:n - SUBLANES, :] = u_scr[0, r:n - SUBLANES + r, :]

    def rows(c, carry):
        base = pl.multiple_of(c * CONV_ROWS, CONV_ROWS)
        acc = jnp.zeros((CONV_ROWS, D_MODEL), F32) + bdw_ref[...]
        for tap in range(CONV_WIDTH):
            a, r = divmod(CONV_HALO - HIST + tap, SUBLANES)
            acc = acc + wdw_ref[tap:tap + 1, :] * u_scr[r, pl.ds(base + a * SUBLANES, CONV_ROWS), :]
        act_ref[pl.ds(base, CONV_ROWS), :] = _ln_swish(acc, lng_ref[...], lnb_ref[...]).astype(BF16)
        return carry

    lax.fori_loop(0, CONV_TILE // CONV_ROWS, rows, 0)

    @pl.when(j == pl.num_programs(1) - 1)
    def _():
        tail_ref[0] = u_scr[0, CONV_TILE:CONV_TILE + CONV_HALO, :]


def _conv_prompt(x_all, g, w1, b1, wdw, bdw, lng, lnb):
    nt = SEQ // CONV_TILE
    per_halo = CONV_TILE // CONV_HALO
    const = lambda b, j: (0, 0)
    return pl.pallas_call(
        _conv_prompt_kernel,
        grid=(BATCH, nt),
        in_specs=[pl.BlockSpec((CONV_HALO, D_MODEL),
                               lambda b, j: (jnp.maximum((b * nt + j) * per_halo - 1, 0), 0)),
                  pl.BlockSpec((CONV_TILE, D_MODEL), lambda b, j: (b * nt + j, 0)),
                  pl.BlockSpec((1, D_MODEL), const),
                  pl.BlockSpec((D_MODEL, 2 * D_MODEL), const),
                  pl.BlockSpec((1, 2 * D_MODEL), const),
                  pl.BlockSpec((CONV_WIDTH, D_MODEL), const),
                  pl.BlockSpec((1, D_MODEL), const),
                  pl.BlockSpec((1, D_MODEL), const),
                  pl.BlockSpec((1, D_MODEL), const)],
        out_specs=[pl.BlockSpec((CONV_TILE, D_MODEL), lambda b, j: (b * nt + j, 0)),
                   pl.BlockSpec((1, CONV_HALO, D_MODEL), lambda b, j: (b, 0, 0))],
        out_shape=[jax.ShapeDtypeStruct((T_PROMPT, D_MODEL), BF16),
                   jax.ShapeDtypeStruct((BATCH, CONV_HALO, D_MODEL), F32)],
        scratch_shapes=[pltpu.VMEM((SUBLANES, CONV_HALO + CONV_TILE, D_MODEL), F32)],
        compiler_params=_params("parallel", "arbitrary"),
        name="conv_prompt",
    )(x_all, x_all, g, w1, b1, wdw, bdw, lng, lnb)


SAMPLE_EXT = 40


def _conv_sample_kernel(x_ref, st_ref, g_ref, w1_ref, b1_ref, wsh_ref, bdw_ref, lng_ref, lnb_ref,
                        act_ref, u_ref):
    u = _glu(_rms(x_ref[...], g_ref[...]).astype(BF16), w1_ref, b1_ref)
    u_ref[...] = u
    pad = jnp.zeros((SAMPLE_EXT - HIST - DEC_SEQ, D_MODEL), F32)
    rows = []
    for b in range(SAMPLE_BATCH_BLOCK):
        ext = jnp.concatenate([st_ref[b], u[b * DEC_SEQ:(b + 1) * DEC_SEQ, :], pad], axis=0)
        for t in range(DEC_SEQ):
            rows.append(jnp.sum(wsh_ref[t] * ext, axis=0, keepdims=True))
    c = jnp.concatenate(rows, axis=0) + bdw_ref[...]
    act_ref[...] = _ln_swish(c, lng_ref[...], lnb_ref[...]).astype(BF16)


def _conv_sample(x_all, state, g, w1, b1, wsh, bdw, lng, lnb):
    bb = SAMPLE_BATCH_BLOCK
    rows = bb * DEC_SEQ
    first = T_PROMPT // rows
    const = lambda i: (0, 0)
    return pl.pallas_call(
        _conv_sample_kernel,
        grid=(DEC_BATCH // bb,),
        in_specs=[pl.BlockSpec((rows, D_MODEL), lambda i: (first + i, 0)),
                  pl.BlockSpec((bb, HIST, D_MODEL), lambda i: (i, 0, 0)),
                  pl.BlockSpec((1, D_MODEL), const),
                  pl.BlockSpec((D_MODEL, 2 * D_MODEL), const),
                  pl.BlockSpec((1, 2 * D_MODEL), const),
                  pl.BlockSpec((DEC_SEQ, SAMPLE_EXT, D_MODEL), lambda i: (0, 0, 0)),
                  pl.BlockSpec((1, D_MODEL), const),
                  pl.BlockSpec((1, D_MODEL), const),
                  pl.BlockSpec((1, D_MODEL), const)],
        out_specs=[pl.BlockSpec((rows, D_MODEL), lambda i: (i, 0)),
                   pl.BlockSpec((rows, D_MODEL), lambda i: (i, 0))],
        out_shape=[jax.ShapeDtypeStruct((T_SAMPLE, D_MODEL), BF16),
                   jax.ShapeDtypeStruct((T_SAMPLE, D_MODEL), F32)],
        compiler_params=_params("parallel"),
        name="conv_sample",
    )(x_all, state, g, w1, b1, wsh, bdw, lng, lnb)


def _router_weights(w_gr, b_gr, w_er, b_er):
    w = jnp.zeros((D_MODEL, ROUTER_COLS), F32)
    w = w.at[:, :N_EXPERTS].set(w_er).at[:, N_EXPERTS:N_EXPERTS + N_GROUPS].set(w_gr)
    b = jnp.zeros((1, ROUTER_COLS), F32)
    b = b.at[0, :N_EXPERTS].set(b_er).at[0, N_EXPERTS:N_EXPERTS + N_GROUPS].set(b_gr)
    return w, b


def _shifted_taps(w_dw):
    frames = [jnp.pad(w_dw, ((t, SAMPLE_EXT - CONV_WIDTH - t), (0, 0))) for t in range(DEC_SEQ)]
    return jnp.stack(frames)


_NB = SEQ // BLOCK
_SAMPLE_BLOCKS = [T_PROMPT // BLOCK + i for i in range(T_SAMPLE // BLOCK)]
PRECISE_KV_BLOCKS = [b * _NB + i for b in range(BATCH) for i in (_NB - 2, _NB - 1)] + _SAMPLE_BLOCKS
PRECISE_Q_BLOCKS = [b * _NB + _NB - 1 for b in range(BATCH)] + _SAMPLE_BLOCKS
_PER_TILE = MOE_TILE // BLOCK
PATCH_SRC = [(b if s == _PER_TILE - 1 else -1) for b in range(BATCH) for s in range(_PER_TILE)] + list(
    range(BATCH, BATCH + len(_SAMPLE_BLOCKS)))
PATCH_TOK = [PRECISE_Q_BLOCKS[max(s, 0)] for s in PATCH_SRC]
assert _NB >= 2 and WIN_BUF == BLOCK and T_SAMPLE % MOE_TILE == 0 and SEQ % MOE_TILE == 0


def kernel(x_prompt, x_sample, cache_k, cache_v, state_conv, w_qkv, w_o, attn_sinks, w_pw1, b_pw1, w_dw, b_dw, conv_ln_g, conv_ln_b, w_pw2, b_pw2, norm_mix_g, norm_ffn_g, w_group_router, b_group_router, w_expert_router, b_expert_router, w_gate, w_up, w_down, final_norm_g):
    row = lambda v: v.reshape(1, -1)
    x = jnp.concatenate([x_prompt.reshape(T_PROMPT, D_MODEL), x_sample.reshape(T_SAMPLE, D_MODEL)], axis=0)
    outs = {}
    for layer in range(DEPTH):
        j = layer // 2
        g_ffn = row(norm_ffn_g[layer])
        wr, br = _router_weights(w_group_router[layer], b_group_router[layer],
                                 w_expert_router[layer], b_expert_router[layer])
        if layer % 2 == 0:
            g_mix = row(norm_mix_g[layer])
            ck = cache_k[j].reshape(DEC_BATCH, WIN_BUF, KV_DIM)
            cv = cache_v[j].reshape(DEC_BATCH, WIN_BUF, KV_DIM)
            q, kv = _qkv(x, g_mix, w_qkv[j].astype(BF16))
            o_p = _attn_prompt(attn_sinks[j], q, kv)
            o_s = _attn_sample(attn_sinks[j], q, kv, ck, cv, T_PROMPT, False)
            mixed = jnp.concatenate([o_p, o_s], axis=0)
            b_out = jnp.zeros((1, D_MODEL), F32)

            q_c, kv_c = _qkv_precise(jnp.asarray(PRECISE_KV_BLOCKS, jnp.int32), x, g_mix, w_qkv[j])
            n_prompt_rows = 2 * BATCH * BLOCK
            o_cp = _attn_prompt_last_precise(attn_sinks[j], q_c, kv_c)
            o_cs = _attn_sample(attn_sinks[j], q_c, kv_c, ck, cv, n_prompt_rows, True)
            patch = _post_precise(jnp.asarray(PATCH_SRC, jnp.int32), jnp.asarray(PATCH_TOK, jnp.int32),
                                  jnp.concatenate([o_cp, o_cs], axis=0), w_o[j], b_out, x, g_ffn, wr, br)
            x1, xs, meta, counts = _post(mixed, w_o[j].astype(BF16), b_out, x, g_ffn, wr.astype(BF16), br, patch)

            kv_p = kv_c[:n_prompt_rows].reshape(BATCH, 2, BLOCK, 2 * KV_DIM)[:, 1]
            kv_s = kv_c[n_prompt_rows:].reshape(DEC_BATCH, DEC_SEQ, 2 * KV_DIM)
            shape5 = lambda a: a.reshape(a.shape[0], WIN_BUF, N_KV_HEADS, HEAD_DIM)
            outs.setdefault("k_p", []).append(shape5(kv_p[..., :KV_DIM]))
            outs.setdefault("v_p", []).append(shape5(kv_p[..., KV_DIM:]))
            outs.setdefault("k_s", []).append(shape5(jnp.concatenate([ck[:, DEC_SEQ:], kv_s[..., :KV_DIM]], axis=1)))
            outs.setdefault("v_s", []).append(shape5(jnp.concatenate([cv[:, DEC_SEQ:], kv_s[..., KV_DIM:]], axis=1)))
        else:
            g = row(norm_mix_g[layer])
            w1, b1 = w_pw1[j].astype(BF16), row(b_pw1[j])
            bdw, lng, lnb = row(b_dw[j]), row(conv_ln_g[j]), row(conv_ln_b[j])
            act_p, tail = _conv_prompt(x, g, w1, b1, w_dw[j], bdw, lng, lnb)
            act_s, u_s = _conv_sample(x, state_conv[j], g, w1, b1, _shifted_taps(w_dw[j]), bdw, lng, lnb)
            mixed = jnp.concatenate([act_p, act_s], axis=0)
            x1, xs, meta, counts = _post(mixed, w_pw2[j].astype(BF16), row(b_pw2[j]), x, g_ffn,
                                         wr.astype(BF16), br)
            outs.setdefault("c_p", []).append(tail[:, CONV_HALO - HIST:])
            u_s = u_s.reshape(DEC_BATCH, DEC_SEQ, D_MODEL)
            outs.setdefault("c_s", []).append(jnp.concatenate([state_conv[j][:, DEC_SEQ:], u_s], axis=1))
        x = _moe(xs, meta, counts, x1, w_gate[layer], w_up[layer], w_down[layer], row(final_norm_g),
                 final=layer == DEPTH - 1)
    y = x
    y_prompt = y[:T_PROMPT].reshape(BATCH, SEQ, D_MODEL)
    y_sample = y[T_PROMPT:].reshape(DEC_BATCH, DEC_SEQ, D_MODEL)
    return (y_prompt, y_sample, jnp.stack(outs["k_p"]), jnp.stack(outs["v_p"]), jnp.stack(outs["c_p"]),
            jnp.stack(outs["k_s"]), jnp.stack(outs["v_s"]), jnp.stack(outs["c_s"]))
```

```python
import functools

import jax
import jax.numpy as jnp
from jax import lax
from jax.experimental import pallas as pl
from jax.experimental.pallas import tpu as pltpu

D_MODEL = 1024
BATCH = 4
SEQ = 4096
DEPTH = 2
DEC_BATCH = 128
DEC_SEQ = 4
HEAD_DIM = 64
N_HEADS = 16
N_KV_HEADS = 4
GQA_GROUP = 4
WINDOW = 128
BLOCK = 128
WIN_BUF = 128
CONV_WIDTH = 31
HIST = CONV_WIDTH - 1
N_GROUPS = 4
EXPERTS_PER_GROUP = 8
N_EXPERTS = 32
D_EXPERT = 256
EPS = 1e-6
NEG_INF = -1e30

T_PROMPT = BATCH * SEQ
T_SAMPLE = DEC_BATCH * DEC_SEQ
T_ALL = T_PROMPT + T_SAMPLE
KV_DIM = N_KV_HEADS * HEAD_DIM
QKV_OUT = D_MODEL + 2 * KV_DIM

LANES = 128
SUBLANES = 8
TOKEN_TILE = 512
CONV_TILE = 512
CONV_ROWS = 32
TAP_UNROLL = SUBLANES
CONV_HALO = 32
SAMPLE_BATCH_BLOCK = 8
VMEM_LIMIT = 48 * 1024 * 1024

F32 = jnp.float32
BF16 = jnp.bfloat16


def _params(*sem):
    return pltpu.CompilerParams(dimension_semantics=sem, vmem_limit_bytes=VMEM_LIMIT)


def _rms(x, g):
    return x * lax.rsqrt(jnp.mean(x * x, axis=-1, keepdims=True) + EPS) * g


def _mm(a, b, precise, contract_b=0):
    dims = (((1,), (contract_b,)), ((), ()))
    if precise:
        return lax.dot_general(a.astype(F32), b.astype(F32), dims, precision=lax.Precision.HIGHEST,
                               preferred_element_type=F32)
    return lax.dot_general(a.astype(BF16), b.astype(BF16), dims, preferred_element_type=F32)


def _qkv_kernel(x_ref, g_ref, w_ref, q_ref, kv_ref, *, precise):
    qkv = _mm(_rms(x_ref[...], g_ref[...]), w_ref[...], precise)
    q_ref[...] = (qkv[:, :D_MODEL] * (HEAD_DIM ** -0.5)).astype(q_ref.dtype)
    kv_ref[...] = qkv[:, D_MODEL:]


def _qkv_precise_kernel(tbl_ref, x_ref, g_ref, w_ref, q_ref, kv_ref):
    del tbl_ref
    _qkv_kernel(x_ref, g_ref, w_ref, q_ref, kv_ref, precise=True)


def _qkv_precise(blocks, x, g, w):
    n = blocks.shape[0]
    return pl.pallas_call(
        _qkv_precise_kernel,
        grid_spec=pltpu.PrefetchScalarGridSpec(
            num_scalar_prefetch=1,
            grid=(n,),
            in_specs=[pl.BlockSpec((BLOCK, D_MODEL), lambda i, t: (t[i], 0)),
                      pl.BlockSpec((1, D_MODEL), lambda i, t: (0, 0)),
                      pl.BlockSpec((D_MODEL, QKV_OUT), lambda i, t: (0, 0))],
            out_specs=[pl.BlockSpec((BLOCK, D_MODEL), lambda i, t: (i, 0)),
                       pl.BlockSpec((BLOCK, 2 * KV_DIM), lambda i, t: (i, 0))],
        ),
        out_shape=[jax.ShapeDtypeStruct((n * BLOCK, D_MODEL), F32),
                   jax.ShapeDtypeStruct((n * BLOCK, 2 * KV_DIM), F32)],
        compiler_params=_params("arbitrary"),
        name="qkv_proj_precise",
    )(blocks, x, g, w)


def _qkv(x, g, w):
    n = x.shape[0] // TOKEN_TILE
    return pl.pallas_call(
        functools.partial(_qkv_kernel, precise=False),
        grid=(n,),
        in_specs=[pl.BlockSpec((TOKEN_TILE, D_MODEL), lambda i: (i, 0)),
                  pl.BlockSpec((1, D_MODEL), lambda i: (0, 0)),
                  pl.BlockSpec((D_MODEL, QKV_OUT), lambda i: (0, 0))],
        out_specs=[pl.BlockSpec((TOKEN_TILE, D_MODEL), lambda i: (i, 0)),
                   pl.BlockSpec((TOKEN_TILE, 2 * KV_DIM), lambda i: (i, 0))],
        out_shape=[jax.ShapeDtypeStruct((x.shape[0], D_MODEL), BF16),
                   jax.ShapeDtypeStruct((x.shape[0], 2 * KV_DIM), F32)],
        compiler_params=_params("parallel"),
        name="qkv_proj",
    )(x, g, w)


def _alibi_slope(head):
    return 2.0 ** (-8.0 * (head + 1) / N_HEADS)


def _q_col(kvh, g):
    return (g * N_KV_HEADS + kvh) * HEAD_DIM


Q_PERM = [(kvh * GQA_GROUP + g) * HEAD_DIM + d
          for g in range(GQA_GROUP) for kvh in range(N_KV_HEADS) for d in range(HEAD_DIM)]


def _softmax_pv(s, dist_f, valid, slope, sink, v, precise):
    logits = jnp.where(valid, s - slope * dist_f, NEG_INF)
    m = jnp.maximum(jnp.max(logits, axis=-1, keepdims=True), sink)
    p = jnp.exp(logits - m)
    denom = jnp.sum(p, axis=-1, keepdims=True) + jnp.exp(sink - m)
    return _mm(p, v, precise) / denom


def _attn_prompt_kernel(sink_ref, q_ref, kvp_ref, kvc_ref, o_ref, *, precise, first_block):
    q = q_ref[...]
    kv = jnp.concatenate([kvp_ref[...], kvc_ref[...]], axis=0)
    kv = kv if precise else kv.astype(BF16)
    qi = lax.broadcasted_iota(jnp.int32, (BLOCK, 2 * BLOCK), 0)
    kj = lax.broadcasted_iota(jnp.int32, (BLOCK, 2 * BLOCK), 1)
    dist = qi + BLOCK - kj
    valid = (dist >= 0) & (dist < WINDOW) & ((kj >= BLOCK) | jnp.logical_not(first_block()))
    dist_f = dist.astype(F32)
    for kvh in range(N_KV_HEADS):
        k_h = kv[:, kvh * HEAD_DIM:(kvh + 1) * HEAD_DIM]
        v_h = kv[:, KV_DIM + kvh * HEAD_DIM:KV_DIM + (kvh + 1) * HEAD_DIM]
        for g in range(GQA_GROUP):
            head = kvh * GQA_GROUP + g
            col = _q_col(kvh, g)
            s = _mm(q[:, col:col + HEAD_DIM], k_h, precise, contract_b=1)
            o = _softmax_pv(s, dist_f, valid, _alibi_slope(head), sink_ref[head], v_h, precise)
            o_ref[:, col:col + HEAD_DIM] = o.astype(o_ref.dtype)


def _attn_prompt_last_precise(sinks, q_c, kv_c):
    return pl.pallas_call(
        functools.partial(_attn_prompt_kernel, precise=True, first_block=lambda: SEQ // BLOCK == 1),
        grid_spec=pltpu.PrefetchScalarGridSpec(
            num_scalar_prefetch=1,
            grid=(BATCH,),
            in_specs=[pl.BlockSpec((BLOCK, D_MODEL), lambda b, s: (2 * b + 1, 0)),
                      pl.BlockSpec((BLOCK, 2 * KV_DIM), lambda b, s: (2 * b, 0)),
                      pl.BlockSpec((BLOCK, 2 * KV_DIM), lambda b, s: (2 * b + 1, 0))],
            out_specs=pl.BlockSpec((BLOCK, D_MODEL), lambda b, s: (b, 0)),
        ),
        out_shape=jax.ShapeDtypeStruct((BATCH * BLOCK, D_MODEL), F32),
        compiler_params=_params("arbitrary"),
        name="attn_prompt_precise",
    )(sinks, q_c, kv_c, kv_c)


def _attn_prompt(sinks, q_all, kv_all):
    nb = SEQ // BLOCK
    return pl.pallas_call(
        functools.partial(_attn_prompt_kernel, precise=False, first_block=lambda: pl.program_id(1) == 0),
        grid_spec=pltpu.PrefetchScalarGridSpec(
            num_scalar_prefetch=1,
            grid=(BATCH, nb),
            in_specs=[pl.BlockSpec((BLOCK, D_MODEL), lambda b, i, s: (b * nb + i, 0)),
                      pl.BlockSpec((BLOCK, 2 * KV_DIM), lambda b, i, s: (b * nb + jnp.maximum(i - 1, 0), 0)),
                      pl.BlockSpec((BLOCK, 2 * KV_DIM), lambda b, i, s: (b * nb + i, 0))],
            out_specs=pl.BlockSpec((BLOCK, D_MODEL), lambda b, i, s: (b * nb + i, 0)),
        ),
        out_shape=jax.ShapeDtypeStruct((T_ALL, D_MODEL), BF16),
        compiler_params=_params("parallel", "parallel"),
        name="attn_prompt",
    )(sinks, q_all, kv_all, kv_all)


SAMPLE_KEYS = WIN_BUF + 16


SEQ_Q_ROWS = DEC_SEQ * GQA_GROUP
SEQ_S_ROWS = N_KV_HEADS * SEQ_Q_ROWS


def _attn_sample_kernel(*refs, precise):
    q_ref, kvn_ref, ck_ref, cv_ref, hp_ref = refs[:5]
    o_ref = refs[-1]
    t_q = (lax.broadcasted_iota(jnp.int32, (SEQ_S_ROWS, SAMPLE_KEYS), 0) % SEQ_Q_ROWS) // GQA_GROUP
    s_k = lax.broadcasted_iota(jnp.int32, (SEQ_S_ROWS, SAMPLE_KEYS), 1)
    dist = WIN_BUF + t_q - s_k
    valid = (dist >= 0) & (dist < WINDOW) & (s_k < WIN_BUF + DEC_SEQ)
    dist_f = dist.astype(F32)
    slope, sink = hp_ref[:, 0:1], hp_ref[:, 1:2]
    col_head = lax.broadcasted_iota(jnp.int32, (SEQ_Q_ROWS, KV_DIM), 1) // HEAD_DIM
    pad = jnp.zeros((SAMPLE_KEYS - WIN_BUF - DEC_SEQ, KV_DIM), F32)
    for b in range(SAMPLE_BATCH_BLOCK):
        q_b = q_ref[b * SEQ_Q_ROWS:(b + 1) * SEQ_Q_ROWS, :]
        kvn = kvn_ref[b * DEC_SEQ:(b + 1) * DEC_SEQ, :]
        k_all = jnp.concatenate([ck_ref[b], kvn[:, :KV_DIM], pad], axis=0)
        v_all = jnp.concatenate([cv_ref[b], kvn[:, KV_DIM:], pad], axis=0)
        q_heads = jnp.concatenate([jnp.where(col_head == kvh, q_b, jnp.zeros_like(q_b))
                                   for kvh in range(N_KV_HEADS)], axis=0)
        s = _mm(q_heads, k_all, precise, contract_b=1)
        o = _softmax_pv(s, dist_f, valid, slope, sink, v_all, precise)
        out = jnp.zeros((SEQ_Q_ROWS, KV_DIM), F32)
        for kvh in range(N_KV_HEADS):
            out = jnp.where(col_head == kvh, o[kvh * SEQ_Q_ROWS:(kvh + 1) * SEQ_Q_ROWS, :], out)
        o_ref[b * SEQ_Q_ROWS:(b + 1) * SEQ_Q_ROWS, :] = out.astype(o_ref.dtype)


def _attn_sample(head_params, q_all, kv_all, cache_k, cache_v, first_row, precise, dest=None):
    bb = SAMPLE_BATCH_BLOCK
    first = first_row // (bb * DEC_SEQ)
    q_rows = q_all.reshape(-1, KV_DIM)
    in_specs = [pl.BlockSpec((bb * SEQ_Q_ROWS, KV_DIM), lambda i: (first + i, 0)),
                pl.BlockSpec((bb * DEC_SEQ, 2 * KV_DIM), lambda i: (first + i, 0)),
                pl.BlockSpec((bb, WIN_BUF, KV_DIM), lambda i: (i, 0, 0)),
                pl.BlockSpec((bb, WIN_BUF, KV_DIM), lambda i: (i, 0, 0)),
                pl.BlockSpec((SEQ_S_ROWS, LANES), lambda i: (0, 0))]
    args = [q_rows, kv_all, cache_k, cache_v, head_params]
    if dest is None:
        out_rows, out_first, dtype, aliases = T_SAMPLE, 0, F32 if precise else BF16, {}
    else:
        out_rows, out_first, dtype, aliases = dest.shape[0], first, dest.dtype, {len(args): 0}
        in_specs.append(pl.BlockSpec(memory_space=pl.ANY))
        args.append(dest.reshape(-1, KV_DIM))
    out = pl.pallas_call(
        functools.partial(_attn_sample_kernel, precise=precise),
        grid=(DEC_BATCH // bb,),
        in_specs=in_specs,
        out_specs=pl.BlockSpec((bb * SEQ_Q_ROWS, KV_DIM), lambda i: (out_first + i, 0)),
        out_shape=jax.ShapeDtypeStruct((out_rows * GQA_GROUP, KV_DIM), dtype),
        input_output_aliases=aliases,
        compiler_params=_params("parallel"),
        name="attn_sample_precise" if precise else "attn_sample",
    )(*args)
    return out.reshape(out_rows, D_MODEL)


ROUTER_COLS = LANES


def _route(logits):
    lane = lax.broadcasted_iota(jnp.int32, logits.shape, 1)
    is_group = (lane >= N_EXPERTS) & (lane < N_EXPERTS + N_GROUPS)
    gl = jnp.where(is_group, logits, -jnp.inf)
    g_max = jnp.max(gl, axis=-1, keepdims=True)
    g_idx = jnp.min(jnp.where(gl == g_max, lane - N_EXPERTS, N_GROUPS), axis=-1, keepdims=True)
    g_w = 1.0 / jnp.sum(jnp.exp(gl - g_max), axis=-1, keepdims=True)
    in_group = (lane < N_EXPERTS) & ((lane // EXPERTS_PER_GROUP) == g_idx)
    el = jnp.where(in_group, logits, -jnp.inf)
    m1 = jnp.max(el, axis=-1, keepdims=True)
    i1 = jnp.min(jnp.where(el == m1, lane, ROUTER_COLS), axis=-1, keepdims=True)
    el2 = jnp.where(lane == i1, -jnp.inf, el)
    m2 = jnp.max(el2, axis=-1, keepdims=True)
    i2 = jnp.min(jnp.where(el2 == m2, lane, ROUTER_COLS), axis=-1, keepdims=True)
    r = jnp.exp(m2 - m1)
    w1 = g_w / (1.0 + r)
    w2 = g_w * r / (1.0 + r)
    return i1, i2, w1, w2


MOE_TILE = TOKEN_TILE
ROW_ALIGN = 16
SLOTS = 2 * MOE_TILE + N_EXPERTS * ROW_ALIGN
X_COLS = D_MODEL + LANES
N_TILES = T_ALL // MOE_TILE
ITEM_ROWS = 1024
SUB_ROWS = 256
MAX_ITEMS = N_TILES * SLOTS // ITEM_ROWS + N_EXPERTS


def _split3(w):
    hi = w.astype(BF16)
    r = w - hi.astype(F32)
    mid = r.astype(BF16)
    lo = (r - mid.astype(F32)).astype(BF16)
    return hi.astype(F32), mid.astype(F32), lo.astype(F32)


def _lane_pack(lane, cols):
    out = jnp.zeros(lane.shape, F32)
    for k, c in enumerate(cols):
        out = jnp.where(lane == k, c, out)
    return out


def _post_kernel(*refs, patched):
    if patched:
        (a_ref, w_ref, b_ref, x_ref, g_ref, wr_ref, br_ref, px1_ref, plg_ref,
         x1_ref, xs_ref, meta_ref, cnt_ref) = refs
    else:
        a_ref, w_ref, b_ref, x_ref, g_ref, wr_ref, br_ref, x1_ref, xs_ref, meta_ref, cnt_ref = refs
    tile = pl.program_id(0)
    x1 = x_ref[...] + (_mm(a_ref[...], w_ref[...], False) + b_ref[...])
    if patched:
        row = lax.broadcasted_iota(jnp.int32, (MOE_TILE, 1), 0)
        tiles_per_seq = SEQ // MOE_TILE
        use = (tile >= T_PROMPT // MOE_TILE) | (
            (tile % tiles_per_seq == tiles_per_seq - 1) & (row >= MOE_TILE - BLOCK))
        x1 = jnp.where(use, px1_ref[...], x1)
    x1_ref[...] = x1
    h = _rms(x1, g_ref[...])
    logits = _mm(h, wr_ref[...], False) + br_ref[...]
    if patched:
        logits = jnp.where(use, plg_ref[...], logits)
    i1, i2, w1, w2 = _route(logits)

    lane = lax.broadcasted_iota(jnp.int32, (MOE_TILE, LANES), 1)
    a1, a2 = lane == i1, lane == i2
    assigned = jnp.where(a1 | a2, 1.0, 0.0).astype(BF16)
    r_t = lax.broadcasted_iota(jnp.int32, (MOE_TILE, MOE_TILE), 0)
    c_t = lax.broadcasted_iota(jnp.int32, (MOE_TILE, MOE_TILE), 1)
    before = jnp.where(c_t < r_t, 1.0, 0.0).astype(BF16)
    rank = jnp.dot(before, assigned, preferred_element_type=F32)
    count = rank[MOE_TILE - 1:, :] + assigned[MOE_TILE - 1:, :].astype(F32)
    padded = jnp.floor((count + (ROW_ALIGN - 1)) * (1.0 / ROW_ALIGN)) * ROW_ALIGN
    r_e = lax.broadcasted_iota(jnp.int32, (LANES, LANES), 0)
    c_e = lax.broadcasted_iota(jnp.int32, (LANES, LANES), 1)
    lower_experts = jnp.where(r_e < c_e, 1.0, 0.0).astype(BF16)
    start = jnp.dot(jnp.broadcast_to(padded, (8, LANES)).astype(BF16), lower_experts,
                    preferred_element_type=F32)[0:1, :]
    pos = start + rank
    s1 = jnp.sum(jnp.where(a1, pos, 0.0), axis=-1, keepdims=True)
    s2 = jnp.sum(jnp.where(a2, pos, 0.0), axis=-1, keepdims=True)
    meta = _lane_pack(lane, [s1, s2])
    meta_ref[...] = meta
    cnt_ref[0] = jnp.broadcast_to(padded, (8, LANES)).astype(jnp.int32)

    slots_t = meta.T
    s_iota = lax.broadcasted_iota(jnp.int32, (SLOTS, MOE_TILE), 0)
    p1 = jnp.where(s_iota == slots_t[0:1, :].astype(jnp.int32), 1.0, 0.0).astype(BF16)
    p2 = jnp.where(s_iota == slots_t[1:2, :].astype(jnp.int32), 1.0, 0.0).astype(BF16)
    xs_ref[:, :D_MODEL] = jnp.dot(p1 + p2, h.astype(BF16), preferred_element_type=F32).astype(BF16)
    wcols = (jnp.dot(p1, _lane_pack(lane, _split3(w1)).astype(BF16), preferred_element_type=F32)
             + jnp.dot(p2, _lane_pack(lane, _split3(w2)).astype(BF16), preferred_element_type=F32))
    xs_ref[:, D_MODEL:] = wcols.astype(BF16)


def _post(a, w, b, x, g, wr, br, patch=None):
    tile = lambda i: (i, 0)
    const = lambda i: (0, 0)
    in_specs = [pl.BlockSpec((MOE_TILE, D_MODEL), tile),
                pl.BlockSpec((D_MODEL, D_MODEL), const),
                pl.BlockSpec((1, D_MODEL), const),
                pl.BlockSpec((MOE_TILE, D_MODEL), tile),
                pl.BlockSpec((1, D_MODEL), const),
                pl.BlockSpec((D_MODEL, ROUTER_COLS), const),
                pl.BlockSpec((1, ROUTER_COLS), const)]
    args = [a, w, b, x, g, wr, br]
    if patch is not None:
        tiles_per_seq = SEQ // MOE_TILE
        pidx = lambda i: (jnp.where(i >= BATCH * tiles_per_seq, i - BATCH * (tiles_per_seq - 1),
                                    i // tiles_per_seq), 0)
        in_specs += [pl.BlockSpec((MOE_TILE, D_MODEL), pidx), pl.BlockSpec((MOE_TILE, ROUTER_COLS), pidx)]
        args += list(patch)
    return pl.pallas_call(
        functools.partial(_post_kernel, patched=patch is not None),
        grid=(N_TILES,),
        in_specs=in_specs,
        out_specs=[pl.BlockSpec((MOE_TILE, D_MODEL), tile),
                   pl.BlockSpec((SLOTS, X_COLS), tile),
                   pl.BlockSpec((MOE_TILE, LANES), tile),
                   pl.BlockSpec((1, 8, LANES), lambda i: (i, 0, 0))],
        out_shape=[jax.ShapeDtypeStruct(x.shape, F32),
                   jax.ShapeDtypeStruct((N_TILES * SLOTS, X_COLS), BF16),
                   jax.ShapeDtypeStruct((x.shape[0], LANES), F32),
                   jax.ShapeDtypeStruct((N_TILES, 8, LANES), jnp.int32)],
        compiler_params=_params("parallel"),
        name="mixer_out_route_sort",
    )(*args)


def _work_plan(counts):
    i32 = lambda v: v.astype(jnp.int32)
    tile_start = jnp.cumsum(counts, axis=1) - counts
    tile_rows = jnp.sum(counts, axis=1)
    seg_rows = jnp.sum(counts, axis=0)
    offset = (jnp.cumsum(counts, axis=0) - counts).T
    n_items_e = (seg_rows + ITEM_ROWS - 1) // ITEM_ROWS
    item_end = jnp.cumsum(n_items_e)
    m = jnp.arange(MAX_ITEMS, dtype=jnp.int32)
    expert = jnp.minimum(jnp.sum(m[:, None] >= item_end[None, :], axis=1), N_EXPERTS - 1)
    base = (m - (item_end - n_items_e)[expert]) * ITEM_ROWS
    off_m = offset[expert]
    lo = jnp.sum(off_m + counts.T[expert] <= base[:, None], axis=1)
    hi = jnp.sum(off_m < (base + ITEM_ROWS)[:, None], axis=1)
    rows = jnp.clip(seg_rows[expert] - base, 0, ITEM_ROWS)
    chunk_row = (jnp.arange(N_TILES, dtype=jnp.int32)[:, None] * SLOTS + tile_start).T
    return (i32(expert), i32(lo), i32(hi), i32(base), i32(rows), i32(item_end[-1]).reshape(1),
            i32(chunk_row.reshape(-1)), i32(counts.T.reshape(-1)), i32(offset.reshape(-1)),
            i32(jnp.arange(N_TILES) * SLOTS + tile_rows), i32(SLOTS - tile_rows))


def _expert_kernel(ie_ref, lo_ref, hi_ref, base_ref, rows_ref, n_ref, crow_ref, clen_ref, coff_ref,
                   tdst_ref, tlen_ref, xs_hbm, wg_ref, wu_ref, wd_ref, y_hbm,
                   xbuf, ybuf, wgu_scr, wd_scr, zbuf, gsem, ssem, zsem):
    k = pl.program_id(0)
    n = n_ref[0]

    def for_chunks(m, fn):
        e, base = ie_ref[m], base_ref[m]
        end = base + rows_ref[m]

        def body(i, carry):
            c = e * N_TILES + i
            first = jnp.maximum(coff_ref[c], base)
            rows = pl.multiple_of(jnp.minimum(coff_ref[c] + clen_ref[c], end) - first, ROW_ALIGN)

            @pl.when(rows > 0)
            def _():
                fn(pl.multiple_of(crow_ref[c] + first - coff_ref[c], ROW_ALIGN),
                   pl.multiple_of(first - base, ROW_ALIGN), rows)
            return carry

        lax.fori_loop(lo_ref[m], hi_ref[m], body, 0)

    def gather(m):
        slot = m % 2
        for_chunks(m, lambda src, dst, rows: pltpu.make_async_copy(
            xs_hbm.at[pl.ds(src, rows)], xbuf.at[slot, pl.ds(dst, rows)], gsem.at[slot]).start())

    def scatter(m):
        slot = m % 2
        for_chunks(m, lambda dst, src, rows: pltpu.make_async_copy(
            ybuf.at[slot, pl.ds(src, rows)], y_hbm.at[pl.ds(dst, rows)], ssem.at[slot]).start())

    def wait_item(buf, sem, m):
        slot = m % 2
        rows = pl.multiple_of(rows_ref[m], ROW_ALIGN)

        @pl.when(rows > 0)
        def _():
            pltpu.make_async_copy(buf.at[slot, pl.ds(0, rows)], buf.at[slot, pl.ds(0, rows)], sem.at[slot]).wait()

    def tails(fn):
        for t in range(N_TILES):
            rows = pl.multiple_of(tlen_ref[t], ROW_ALIGN)

            @pl.when(rows > 0)
            def _():
                fn(pltpu.make_async_copy(zbuf.at[pl.ds(0, rows)],
                                         y_hbm.at[pl.ds(pl.multiple_of(tdst_ref[t], ROW_ALIGN), rows)], zsem))

    @pl.when(k == 0)
    def _():
        xbuf[...] = jnp.zeros_like(xbuf)
        zbuf[...] = jnp.zeros_like(zbuf)
        tails(lambda cp: cp.start())

        @pl.when(n > 0)
        def _():
            gather(0)

    @pl.when(k < n)
    def _():
        @pl.when(k + 1 < n)
        def _():
            gather(k + 1)

        @pl.when((k == 0) | (ie_ref[k] != ie_ref[jnp.maximum(k - 1, 0)]))
        def _():
            wgu_scr[:, :D_EXPERT] = wg_ref[0].astype(BF16)
            wgu_scr[:, D_EXPERT:] = wu_ref[0].astype(BF16)
            wd_scr[...] = wd_ref[0].astype(BF16)

        slot = k % 2
        wait_item(xbuf, gsem, k)

        @pl.when(k >= 2)
        def _():
            wait_item(ybuf, ssem, k - 2)

        for sb in range(ITEM_ROWS // SUB_ROWS):
            @pl.when(sb * SUB_ROWS < rows_ref[k])
            def _():
                r0 = sb * SUB_ROWS
                x = xbuf[slot, r0:r0 + SUB_ROWS, :]
                gu = jnp.dot(x[:, :D_MODEL], wgu_scr[...], preferred_element_type=F32)
                gate, up = gu[:, :D_EXPERT], gu[:, D_EXPERT:]
                hid = (gate * jax.nn.sigmoid(gate) * up).astype(BF16)
                y = jnp.dot(hid, wd_scr[...], preferred_element_type=F32)
                wparts = x[:, D_MODEL:].astype(F32)
                weight = wparts[:, 0:1] + wparts[:, 1:2] + wparts[:, 2:3]
                ybuf[slot, r0:r0 + SUB_ROWS, :] = (weight * y).astype(BF16)

        scatter(k)

        @pl.when(k == n - 1)
        def _():
            @pl.when(k >= 1)
            def _():
                wait_item(ybuf, ssem, k - 1)
            wait_item(ybuf, ssem, k)
            tails(lambda cp: cp.wait())


def _experts(plan, xs, wg, wu, wd):
    wsel = lambda k, *p: (p[0][jnp.minimum(k, jnp.maximum(p[5][0] - 1, 0))], 0, 0)
    any_space = pl.BlockSpec(memory_space=pl.ANY)
    return pl.pallas_call(
        _expert_kernel,
        grid_spec=pltpu.PrefetchScalarGridSpec(
            num_scalar_prefetch=len(plan),
            grid=(MAX_ITEMS,),
            in_specs=[any_space,
                      pl.BlockSpec((1, D_MODEL, D_EXPERT), wsel),
                      pl.BlockSpec((1, D_MODEL, D_EXPERT), wsel),
                      pl.BlockSpec((1, D_EXPERT, D_MODEL), wsel)],
            out_specs=any_space,
            scratch_shapes=[pltpu.VMEM((2, ITEM_ROWS, X_COLS), BF16),
                            pltpu.VMEM((2, ITEM_ROWS, D_MODEL), BF16),
                            pltpu.VMEM((D_MODEL, 2 * D_EXPERT), BF16),
                            pltpu.VMEM((D_EXPERT, D_MODEL), BF16),
                            pltpu.VMEM((MOE_TILE, D_MODEL), BF16),
                            pltpu.SemaphoreType.DMA((2,)),
                            pltpu.SemaphoreType.DMA((2,)),
                            pltpu.SemaphoreType.DMA(())],
        ),
        out_shape=jax.ShapeDtypeStruct((N_TILES * SLOTS, D_MODEL), BF16),
        compiler_params=_params("arbitrary"),
        name="experts",
    )(*plan, xs, wg, wu, wd)


PROMPT_TILES = T_PROMPT // MOE_TILE


def _combine_kernel(y_ref, meta_ref, x1_ref, g_ref, *o_refs, final):
    meta = meta_ref[...]
    s_iota = lax.broadcasted_iota(jnp.int32, (MOE_TILE, SLOTS), 1)
    pick = ((s_iota == meta[:, 0:1].astype(jnp.int32)) | (s_iota == meta[:, 1:2].astype(jnp.int32)))
    moe = jnp.dot(jnp.where(pick, 1.0, 0.0).astype(BF16), y_ref[...], preferred_element_type=F32)
    x2 = x1_ref[...] + moe
    if not final:
        o_refs[0][...] = x2
        return
    y = _rms(x2, g_ref[...])
    is_prompt = pl.program_id(0) < PROMPT_TILES

    @pl.when(is_prompt)
    def _():
        o_refs[0][...] = y

    @pl.when(jnp.logical_not(is_prompt))
    def _():
        o_refs[1][...] = y


def _combine(y_local, meta, x1, g_final, final):
    tile = lambda i: (i, 0)
    if final:
        out_specs = [pl.BlockSpec((MOE_TILE, D_MODEL), lambda i: (jnp.minimum(i, PROMPT_TILES - 1), 0)),
                     pl.BlockSpec((MOE_TILE, D_MODEL), lambda i: (jnp.maximum(i - PROMPT_TILES, 0), 0))]
        out_shape = [jax.ShapeDtypeStruct((T_PROMPT, D_MODEL), F32), jax.ShapeDtypeStruct((T_SAMPLE, D_MODEL), F32)]
    else:
        out_specs, out_shape = pl.BlockSpec((MOE_TILE, D_MODEL), tile), jax.ShapeDtypeStruct(x1.shape, F32)
    return pl.pallas_call(
        functools.partial(_combine_kernel, final=final),
        grid=(N_TILES,),
        in_specs=[pl.BlockSpec((SLOTS, D_MODEL), tile),
                  pl.BlockSpec((MOE_TILE, LANES), tile),
                  pl.BlockSpec((MOE_TILE, D_MODEL), tile),
                  pl.BlockSpec((1, D_MODEL), lambda i: (0, 0))],
        out_specs=out_specs,
        out_shape=out_shape,
        compiler_params=_params("arbitrary" if final else "parallel"),
        name="moe_combine",
    )(y_local, meta, x1, g_final)


def _moe(xs, meta, counts, x1, wg, wu, wd, g_final, final):
    plan = _work_plan(counts[:, 0, :N_EXPERTS])
    return _combine(_experts(plan, xs, wg, wu, wd), meta, x1, g_final, final)


def _post_precise_kernel(src_ref, tok_ref, a_ref, w_ref, b_ref, x_ref, g_ref, wr_ref, br_ref, x1_ref, lg_ref):
    del tok_ref
    filled = src_ref[pl.program_id(0)] >= 0

    @pl.when(filled)
    def _():
        x1 = x_ref[...] + (_mm(a_ref[...], w_ref[...], True) + b_ref[...])
        x1_ref[...] = x1
        lg_ref[...] = _mm(_rms(x1, g_ref[...]), wr_ref[...], True) + br_ref[...]

    @pl.when(jnp.logical_not(filled))
    def _():
        x1_ref[...] = jnp.zeros_like(x1_ref)
        lg_ref[...] = jnp.zeros_like(lg_ref)


def _post_precise(src_blocks, tok_blocks, a_c, w, b, x, g, wr, br):
    n = src_blocks.shape[0]
    const = lambda i, s, t: (0, 0)
    slot = lambda i, s, t: (i, 0)
    return pl.pallas_call(
        _post_precise_kernel,
        grid_spec=pltpu.PrefetchScalarGridSpec(
            num_scalar_prefetch=2,
            grid=(n,),
            in_specs=[pl.BlockSpec((BLOCK, D_MODEL), lambda i, s, t: (jnp.maximum(s[i], 0), 0)),
                      pl.BlockSpec((D_MODEL, D_MODEL), const),
                      pl.BlockSpec((1, D_MODEL), const),
                      pl.BlockSpec((BLOCK, D_MODEL), lambda i, s, t: (t[i], 0)),
                      pl.BlockSpec((1, D_MODEL), const),
                      pl.BlockSpec((D_MODEL, ROUTER_COLS), const),
                      pl.BlockSpec((1, ROUTER_COLS), const)],
            out_specs=[pl.BlockSpec((BLOCK, D_MODEL), slot),
                       pl.BlockSpec((BLOCK, ROUTER_COLS), slot)],
        ),
        out_shape=[jax.ShapeDtypeStruct((n * BLOCK, D_MODEL), F32),
                   jax.ShapeDtypeStruct((n * BLOCK, ROUTER_COLS), F32)],
        compiler_params=_params("arbitrary"),
        name="mixer_out_route_precise",
    )(src_blocks, tok_blocks, a_c, w, b, x, g, wr, br)


def _glu(h, w_ref, b_ref):
    a = jnp.dot(h, w_ref[...], preferred_element_type=F32) + b_ref[...]
    return a[:, :D_MODEL] * jax.nn.sigmoid(a[:, D_MODEL:])


def _ln_swish(c, g, b):
    cc = c - jnp.mean(c, axis=-1, keepdims=True)
    var = jnp.mean(cc * cc, axis=-1, keepdims=True)
    y = cc * lax.rsqrt(var + EPS) * g + b
    return y * jax.nn.sigmoid(y)


def _conv_prompt_kernel(halo_ref, x_ref, g_ref, w1_ref, b1_ref, wdw_ref, bdw_ref, lng_ref, lnb_ref,
                        act_ref, tail_ref, u_scr, c_scr):
    j = pl.program_id(1)
    xh = jnp.concatenate([halo_ref[...], x_ref[...]], axis=0)
    u = _glu(_rms(xh, g_ref[...]).astype(BF16), w1_ref, b1_ref)
    n = CONV_HALO + CONV_TILE
    row = lax.broadcasted_iota(jnp.int32, (n, 1), 0)
    u_scr[0] = jnp.where((row < CONV_HALO) & (j == 0), 0.0, u)
    for r in range(1, SUBLANES):
        u_scr[r, 0:n - SUBLANES, :] = u_scr[0, r:n - SUBLANES + r, :]

    def rows(c, carry):
        base = pl.multiple_of(c * CONV_ROWS, CONV_ROWS)
        pieces = CONV_ROWS // SUBLANES

        def taps(group, count, acc):
            first = pl.multiple_of(base + group * TAP_UNROLL, SUBLANES)
            for tap in range(count):
                a, r = divmod(CONV_HALO - HIST + tap, SUBLANES)
                w = wdw_ref[group * TAP_UNROLL + tap]
                acc = tuple(acc[p] + w * u_scr[r, pl.ds(first + (a + p) * SUBLANES, SUBLANES), :]
                            for p in range(pieces))
            return acc

        acc = tuple(jnp.zeros((SUBLANES, D_MODEL), F32) + bdw_ref[...] for _ in range(pieces))
        groups = CONV_WIDTH // TAP_UNROLL
        acc = lax.fori_loop(0, groups, lambda g, acc: taps(g, TAP_UNROLL, acc), acc)
        acc = taps(groups, CONV_WIDTH - groups * TAP_UNROLL, acc)
        c_scr[pl.ds(base, CONV_ROWS), :] = jnp.concatenate(acc, axis=0)
        return carry

    lax.fori_loop(0, CONV_TILE // CONV_ROWS, rows, 0)
    act_ref[...] = _ln_swish(c_scr[...], lng_ref[...], lnb_ref[...]).astype(BF16)

    @pl.when(j == pl.num_programs(1) - 1)
    def _():
        tail_ref[0] = u_scr[0, CONV_TILE:CONV_TILE + CONV_HALO, :]


def _conv_prompt(x_all, g, w1, b1, wdw, bdw, lng, lnb):
    nt = SEQ // CONV_TILE
    per_halo = CONV_TILE // CONV_HALO
    const = lambda b, j: (0, 0)
    return pl.pallas_call(
        _conv_prompt_kernel,
        grid=(BATCH, nt),
        in_specs=[pl.BlockSpec((CONV_HALO, D_MODEL),
                               lambda b, j: (jnp.maximum((b * nt + j) * per_halo - 1, 0), 0)),
                  pl.BlockSpec((CONV_TILE, D_MODEL), lambda b, j: (b * nt + j, 0)),
                  pl.BlockSpec((1, D_MODEL), const),
                  pl.BlockSpec((D_MODEL, 2 * D_MODEL), const),
                  pl.BlockSpec((1, 2 * D_MODEL), const),
                  pl.BlockSpec((CONV_WIDTH, SUBLANES, D_MODEL), lambda b, j: (0, 0, 0)),
                  pl.BlockSpec((1, D_MODEL), const),
                  pl.BlockSpec((1, D_MODEL), const),
                  pl.BlockSpec((1, D_MODEL), const)],
        out_specs=[pl.BlockSpec((CONV_TILE, D_MODEL), lambda b, j: (b * nt + j, 0)),
                   pl.BlockSpec((1, CONV_HALO, D_MODEL), lambda b, j: (b, 0, 0))],
        out_shape=[jax.ShapeDtypeStruct((T_ALL, D_MODEL), BF16),
                   jax.ShapeDtypeStruct((BATCH, CONV_HALO, D_MODEL), F32)],
        scratch_shapes=[pltpu.VMEM((SUBLANES, CONV_HALO + CONV_TILE, D_MODEL), F32),
                        pltpu.VMEM((CONV_TILE, D_MODEL), F32)],
        compiler_params=_params("parallel", "arbitrary"),
        name="conv_prompt",
    )(x_all, x_all, g, w1, b1, wdw, bdw, lng, lnb)


SAMPLE_EXT = 40


def _conv_sample_kernel(x_ref, st_ref, g_ref, w1_ref, b1_ref, wsh_ref, bdw_ref, lng_ref, lnb_ref, dest_ref,
                        act_ref, u_ref):
    del dest_ref
    u = _glu(_rms(x_ref[...], g_ref[...]).astype(BF16), w1_ref, b1_ref)
    u_ref[...] = u
    pad = jnp.zeros((SAMPLE_EXT - HIST - DEC_SEQ, D_MODEL), F32)
    rows = []
    for b in range(SAMPLE_BATCH_BLOCK):
        ext = jnp.concatenate([st_ref[b], u[b * DEC_SEQ:(b + 1) * DEC_SEQ, :], pad], axis=0)
        for t in range(DEC_SEQ):
            rows.append(jnp.sum(wsh_ref[t] * ext, axis=0, keepdims=True))
    c = jnp.concatenate(rows, axis=0) + bdw_ref[...]
    act_ref[...] = _ln_swish(c, lng_ref[...], lnb_ref[...]).astype(BF16)


def _conv_sample(x_all, state, g, w1, b1, wsh, bdw, lng, lnb, dest):
    bb = SAMPLE_BATCH_BLOCK
    rows = bb * DEC_SEQ
    first = T_PROMPT // rows
    const = lambda i: (0, 0)
    return pl.pallas_call(
        _conv_sample_kernel,
        grid=(DEC_BATCH // bb,),
        in_specs=[pl.BlockSpec((rows, D_MODEL), lambda i: (first + i, 0)),
                  pl.BlockSpec((bb, HIST, D_MODEL), lambda i: (i, 0, 0)),
                  pl.BlockSpec((1, D_MODEL), const),
                  pl.BlockSpec((D_MODEL, 2 * D_MODEL), const),
                  pl.BlockSpec((1, 2 * D_MODEL), const),
                  pl.BlockSpec((DEC_SEQ, SAMPLE_EXT, D_MODEL), lambda i: (0, 0, 0)),
                  pl.BlockSpec((1, D_MODEL), const),
                  pl.BlockSpec((1, D_MODEL), const),
                  pl.BlockSpec((1, D_MODEL), const),
                  pl.BlockSpec(memory_space=pl.ANY)],
        out_specs=[pl.BlockSpec((rows, D_MODEL), lambda i: (first + i, 0)),
                   pl.BlockSpec((rows, D_MODEL), lambda i: (i, 0))],
        out_shape=[jax.ShapeDtypeStruct(dest.shape, dest.dtype),
                   jax.ShapeDtypeStruct((T_SAMPLE, D_MODEL), F32)],
        input_output_aliases={9: 0},
        compiler_params=_params("parallel"),
        name="conv_sample",
    )(x_all, state, g, w1, b1, wsh, bdw, lng, lnb, dest)


def _router_weights(w_gr, b_gr, w_er, b_er):
    w = jnp.zeros((D_MODEL, ROUTER_COLS), F32)
    w = w.at[:, :N_EXPERTS].set(w_er).at[:, N_EXPERTS:N_EXPERTS + N_GROUPS].set(w_gr)
    b = jnp.zeros((1, ROUTER_COLS), F32)
    b = b.at[0, :N_EXPERTS].set(b_er).at[0, N_EXPERTS:N_EXPERTS + N_GROUPS].set(b_gr)
    return w, b


def _shifted_taps(w_dw):
    frames = [jnp.pad(w_dw, ((t, SAMPLE_EXT - CONV_WIDTH - t), (0, 0))) for t in range(DEC_SEQ)]
    return jnp.stack(frames)


_NB = SEQ // BLOCK
_SAMPLE_BLOCKS = [T_PROMPT // BLOCK + i for i in range(T_SAMPLE // BLOCK)]
PRECISE_KV_BLOCKS = [b * _NB + i for b in range(BATCH) for i in (_NB - 2, _NB - 1)] + _SAMPLE_BLOCKS
PRECISE_Q_BLOCKS = [b * _NB + _NB - 1 for b in range(BATCH)] + _SAMPLE_BLOCKS
_PER_TILE = MOE_TILE // BLOCK
PATCH_SRC = [(b if s == _PER_TILE - 1 else -1) for b in range(BATCH) for s in range(_PER_TILE)] + list(
    range(BATCH, BATCH + len(_SAMPLE_BLOCKS)))
PATCH_TOK = [PRECISE_Q_BLOCKS[max(s, 0)] for s in PATCH_SRC]
assert _NB >= 2 and WIN_BUF == BLOCK and T_SAMPLE % MOE_TILE == 0 and SEQ % MOE_TILE == 0


def kernel(x_prompt, x_sample, cache_k, cache_v, state_conv, w_qkv, w_o, attn_sinks, w_pw1, b_pw1, w_dw, b_dw, conv_ln_g, conv_ln_b, w_pw2, b_pw2, norm_mix_g, norm_ffn_g, w_group_router, b_group_router, w_expert_router, b_expert_router, w_gate, w_up, w_down, final_norm_g):
    row = lambda v: v.reshape(1, -1)
    x = jnp.concatenate([x_prompt.reshape(T_PROMPT, D_MODEL), x_sample.reshape(T_SAMPLE, D_MODEL)], axis=0)
    outs = {}
    for layer in range(DEPTH):
        j = layer // 2
        g_ffn = row(norm_ffn_g[layer])
        wr, br = _router_weights(w_group_router[layer], b_group_router[layer],
                                 w_expert_router[layer], b_expert_router[layer])
        if layer % 2 == 0:
            g_mix = row(norm_mix_g[layer])
            ck = cache_k[j].reshape(DEC_BATCH, WIN_BUF, KV_DIM)
            cv = cache_v[j].reshape(DEC_BATCH, WIN_BUF, KV_DIM)
            perm = jnp.asarray(Q_PERM, jnp.int32)
            wqkv = jnp.concatenate([w_qkv[j][:, perm], w_qkv[j][:, D_MODEL:]], axis=1)
            wo = w_o[j][perm, :]
            row_head = jnp.asarray([kvh * GQA_GROUP + g for kvh in range(N_KV_HEADS)
                                    for _ in range(DEC_SEQ) for g in range(GQA_GROUP)], jnp.int32)
            slopes = jnp.asarray([_alibi_slope(h) for h in range(N_HEADS)], F32)
            head_params = jnp.zeros((SEQ_S_ROWS, LANES), F32)
            head_params = head_params.at[:, 0].set(slopes[row_head]).at[:, 1].set(attn_sinks[j][row_head])

            q, kv = _qkv(x, g_mix, wqkv.astype(BF16))
            mixed = _attn_prompt(attn_sinks[j], q, kv)
            mixed = _attn_sample(head_params, q, kv, ck, cv, T_PROMPT, False, dest=mixed)
            b_out = jnp.zeros((1, D_MODEL), F32)

            q_c, kv_c = _qkv_precise(jnp.asarray(PRECISE_KV_BLOCKS, jnp.int32), x, g_mix, wqkv)
            n_prompt_rows = 2 * BATCH * BLOCK
            o_cp = _attn_prompt_last_precise(attn_sinks[j], q_c, kv_c)
            o_cs = _attn_sample(head_params, q_c, kv_c, ck, cv, n_prompt_rows, True)
            patch = _post_precise(jnp.asarray(PATCH_SRC, jnp.int32), jnp.asarray(PATCH_TOK, jnp.int32),
                                  jnp.concatenate([o_cp, o_cs], axis=0), wo, b_out, x, g_ffn, wr, br)
            x1, xs, meta, counts = _post(mixed, wo.astype(BF16), b_out, x, g_ffn, wr.astype(BF16), br, patch)

            kv_p = kv_c[:n_prompt_rows].reshape(BATCH, 2, BLOCK, 2 * KV_DIM)[:, 1]
            kv_s = kv_c[n_prompt_rows:].reshape(DEC_BATCH, DEC_SEQ, 2 * KV_DIM)
            shape5 = lambda a: a.reshape(a.shape[0], WIN_BUF, N_KV_HEADS, HEAD_DIM)
            outs.setdefault("k_p", []).append(shape5(kv_p[..., :KV_DIM]))
            outs.setdefault("v_p", []).append(shape5(kv_p[..., KV_DIM:]))
            outs.setdefault("k_s", []).append(shape5(jnp.concatenate([ck[:, DEC_SEQ:], kv_s[..., :KV_DIM]], axis=1)))
            outs.setdefault("v_s", []).append(shape5(jnp.concatenate([cv[:, DEC_SEQ:], kv_s[..., KV_DIM:]], axis=1)))
        else:
            g = row(norm_mix_g[layer])
            w1, b1 = w_pw1[j].astype(BF16), row(b_pw1[j])
            bdw, lng, lnb = row(b_dw[j]), row(conv_ln_g[j]), row(conv_ln_b[j])
            taps = jnp.broadcast_to(w_dw[j][:, None, :], (CONV_WIDTH, SUBLANES, D_MODEL))
            mixed, tail = _conv_prompt(x, g, w1, b1, taps, bdw, lng, lnb)
            mixed, u_s = _conv_sample(x, state_conv[j], g, w1, b1, _shifted_taps(w_dw[j]), bdw, lng, lnb, mixed)
            x1, xs, meta, counts = _post(mixed, w_pw2[j].astype(BF16), row(b_pw2[j]), x, g_ffn,
                                         wr.astype(BF16), br)
            outs.setdefault("c_p", []).append(tail[:, CONV_HALO - HIST:])
            u_s = u_s.reshape(DEC_BATCH, DEC_SEQ, D_MODEL)
            outs.setdefault("c_s", []).append(jnp.concatenate([state_conv[j][:, DEC_SEQ:], u_s], axis=1))
        x = _moe(xs, meta, counts, x1, w_gate[layer], w_up[layer], w_down[layer], row(final_norm_g),
                 final=layer == DEPTH - 1)
    y_prompt = x[0].reshape(BATCH, SEQ, D_MODEL)
    y_sample = x[1].reshape(DEC_BATCH, DEC_SEQ, D_MODEL)
    return (y_prompt, y_sample, jnp.stack(outs["k_p"]), jnp.stack(outs["v_p"]), jnp.stack(outs["c_p"]),
            jnp.stack(outs["k_s"]), jnp.stack(outs["v_s"]), jnp.stack(outs["c_s"]))
```

```python
import functools

import jax
import jax.numpy as jnp
from jax import lax
from jax.experimental import pallas as pl
from jax.experimental.pallas import tpu as pltpu

D_MODEL = 1024
BATCH = 4
SEQ = 4096
DEPTH = 2
DEC_BATCH = 128
DEC_SEQ = 4
HEAD_DIM = 64
N_HEADS = 16
N_KV_HEADS = 4
GQA_GROUP = 4
WINDOW = 128
BLOCK = 128
WIN_BUF = 128
CONV_WIDTH = 31
HIST = CONV_WIDTH - 1
N_GROUPS = 4
EXPERTS_PER_GROUP = 8
N_EXPERTS = 32
D_EXPERT = 256
EPS = 1e-6
NEG_INF = -1e30

T_PROMPT = BATCH * SEQ
T_SAMPLE = DEC_BATCH * DEC_SEQ
T_ALL = T_PROMPT + T_SAMPLE
KV_DIM = N_KV_HEADS * HEAD_DIM
QKV_OUT = D_MODEL + 2 * KV_DIM

LANES = 128
SUBLANES = 8
TOKEN_TILE = 512
CONV_TILE = 512
CONV_ROWS = 32
TAP_UNROLL = SUBLANES
CONV_HALO = 32
SAMPLE_BATCH_BLOCK = 8
VMEM_LIMIT = 48 * 1024 * 1024

F32 = jnp.float32
BF16 = jnp.bfloat16


def _params(*sem):
    return pltpu.CompilerParams(dimension_semantics=sem, vmem_limit_bytes=VMEM_LIMIT)


def _rms(x, g):
    return x * lax.rsqrt(jnp.mean(x * x, axis=-1, keepdims=True) + EPS) * g


def _mm(a, b, precise, contract_b=0):
    dims = (((1,), (contract_b,)), ((), ()))
    if precise:
        return lax.dot_general(a.astype(F32), b.astype(F32), dims, precision=lax.Precision.HIGHEST,
                               preferred_element_type=F32)
    return lax.dot_general(a.astype(BF16), b.astype(BF16), dims, preferred_element_type=F32)


def _qkv_kernel(x_ref, g_ref, w_ref, q_ref, kv_ref, *, precise):
    qkv = _mm(_rms(x_ref[...], g_ref[...]), w_ref[...], precise)
    q_ref[...] = (qkv[:, :D_MODEL] * (HEAD_DIM ** -0.5)).astype(q_ref.dtype)
    kv_ref[...] = qkv[:, D_MODEL:]


def _qkv_precise_kernel(tbl_ref, x_ref, g_ref, w_ref, q_ref, kv_ref):
    del tbl_ref
    _qkv_kernel(x_ref, g_ref, w_ref, q_ref, kv_ref, precise=True)


def _qkv_precise(blocks, x, g, w):
    n = blocks.shape[0]
    return pl.pallas_call(
        _qkv_precise_kernel,
        grid_spec=pltpu.PrefetchScalarGridSpec(
            num_scalar_prefetch=1,
            grid=(n,),
            in_specs=[pl.BlockSpec((BLOCK, D_MODEL), lambda i, t: (t[i], 0)),
                      pl.BlockSpec((1, D_MODEL), lambda i, t: (0, 0)),
                      pl.BlockSpec((D_MODEL, QKV_OUT), lambda i, t: (0, 0))],
            out_specs=[pl.BlockSpec((BLOCK, D_MODEL), lambda i, t: (i, 0)),
                       pl.BlockSpec((BLOCK, 2 * KV_DIM), lambda i, t: (i, 0))],
        ),
        out_shape=[jax.ShapeDtypeStruct((n * BLOCK, D_MODEL), F32),
                   jax.ShapeDtypeStruct((n * BLOCK, 2 * KV_DIM), F32)],
        compiler_params=_params("arbitrary"),
        name="qkv_proj_precise",
    )(blocks, x, g, w)


def _qkv_join_kernel(xp_ref, xs_ref, g_ref, w_ref, q_ref, kv_ref, x_ref):
    x_ref[...] = jnp.where(pl.program_id(0) < T_PROMPT // TOKEN_TILE, xp_ref[...], xs_ref[...])
    _qkv_kernel(x_ref, g_ref, w_ref, q_ref, kv_ref, precise=False)


def _qkv(x_prompt, x_sample, g, w):
    n_prompt, n_sample = T_PROMPT // TOKEN_TILE, T_SAMPLE // TOKEN_TILE
    tile = lambda i: (i, 0)
    return pl.pallas_call(
        _qkv_join_kernel,
        grid=(n_prompt + n_sample,),
        in_specs=[pl.BlockSpec((TOKEN_TILE, D_MODEL), lambda i: (jnp.minimum(i, n_prompt - 1), 0)),
                  pl.BlockSpec((TOKEN_TILE, D_MODEL), lambda i: (jnp.maximum(i - n_prompt, 0), 0)),
                  pl.BlockSpec((1, D_MODEL), lambda i: (0, 0)),
                  pl.BlockSpec((D_MODEL, QKV_OUT), lambda i: (0, 0))],
        out_specs=[pl.BlockSpec((TOKEN_TILE, D_MODEL), tile),
                   pl.BlockSpec((TOKEN_TILE, 2 * KV_DIM), tile),
                   pl.BlockSpec((TOKEN_TILE, D_MODEL), tile)],
        out_shape=[jax.ShapeDtypeStruct((T_ALL, D_MODEL), BF16),
                   jax.ShapeDtypeStruct((T_ALL, 2 * KV_DIM), F32),
                   jax.ShapeDtypeStruct((T_ALL, D_MODEL), F32)],
        compiler_params=_params("parallel"),
        name="qkv_proj",
    )(x_prompt, x_sample, g, w)


def _alibi_slope(head):
    return 2.0 ** (-8.0 * (head + 1) / N_HEADS)


def _q_col(kvh, g):
    return (g * N_KV_HEADS + kvh) * HEAD_DIM


Q_PERM = [(kvh * GQA_GROUP + g) * HEAD_DIM + d
          for g in range(GQA_GROUP) for kvh in range(N_KV_HEADS) for d in range(HEAD_DIM)]


def _softmax_pv(s, dist_f, valid, slope, sink, v, precise):
    logits = jnp.where(valid, s - slope * dist_f, NEG_INF)
    m = jnp.maximum(jnp.max(logits, axis=-1, keepdims=True), sink)
    p = jnp.exp(logits - m)
    denom = jnp.sum(p, axis=-1, keepdims=True) + jnp.exp(sink - m)
    return _mm(p, v, precise) / denom


def _attn_prompt_kernel(sink_ref, q_ref, kvp_ref, kvc_ref, o_ref, *, precise, first_block):
    q = q_ref[...]
    kv = jnp.concatenate([kvp_ref[...], kvc_ref[...]], axis=0)
    kv = kv if precise else kv.astype(BF16)
    qi = lax.broadcasted_iota(jnp.int32, (BLOCK, 2 * BLOCK), 0)
    kj = lax.broadcasted_iota(jnp.int32, (BLOCK, 2 * BLOCK), 1)
    dist = qi + BLOCK - kj
    valid = (dist >= 0) & (dist < WINDOW) & ((kj >= BLOCK) | jnp.logical_not(first_block()))
    dist_f = dist.astype(F32)
    for kvh in range(N_KV_HEADS):
        k_h = kv[:, kvh * HEAD_DIM:(kvh + 1) * HEAD_DIM]
        v_h = kv[:, KV_DIM + kvh * HEAD_DIM:KV_DIM + (kvh + 1) * HEAD_DIM]
        for g in range(GQA_GROUP):
            head = kvh * GQA_GROUP + g
            col = _q_col(kvh, g)
            s = _mm(q[:, col:col + HEAD_DIM], k_h, precise, contract_b=1)
            o = _softmax_pv(s, dist_f, valid, _alibi_slope(head), sink_ref[head], v_h, precise)
            o_ref[:, col:col + HEAD_DIM] = o.astype(o_ref.dtype)


def _attn_prompt_last_precise(sinks, q_c, kv_c):
    return pl.pallas_call(
        functools.partial(_attn_prompt_kernel, precise=True, first_block=lambda: SEQ // BLOCK == 1),
        grid_spec=pltpu.PrefetchScalarGridSpec(
            num_scalar_prefetch=1,
            grid=(BATCH,),
            in_specs=[pl.BlockSpec((BLOCK, D_MODEL), lambda b, s: (2 * b + 1, 0)),
                      pl.BlockSpec((BLOCK, 2 * KV_DIM), lambda b, s: (2 * b, 0)),
                      pl.BlockSpec((BLOCK, 2 * KV_DIM), lambda b, s: (2 * b + 1, 0))],
            out_specs=pl.BlockSpec((BLOCK, D_MODEL), lambda b, s: (b, 0)),
        ),
        out_shape=jax.ShapeDtypeStruct((BATCH * BLOCK, D_MODEL), F32),
        compiler_params=_params("arbitrary"),
        name="attn_prompt_precise",
    )(sinks, q_c, kv_c, kv_c)


def _attn_prompt(sinks, q_all, kv_all):
    nb = SEQ // BLOCK
    return pl.pallas_call(
        functools.partial(_attn_prompt_kernel, precise=False, first_block=lambda: pl.program_id(1) == 0),
        grid_spec=pltpu.PrefetchScalarGridSpec(
            num_scalar_prefetch=1,
            grid=(BATCH, nb),
            in_specs=[pl.BlockSpec((BLOCK, D_MODEL), lambda b, i, s: (b * nb + i, 0)),
                      pl.BlockSpec((BLOCK, 2 * KV_DIM), lambda b, i, s: (b * nb + jnp.maximum(i - 1, 0), 0)),
                      pl.BlockSpec((BLOCK, 2 * KV_DIM), lambda b, i, s: (b * nb + i, 0))],
            out_specs=pl.BlockSpec((BLOCK, D_MODEL), lambda b, i, s: (b * nb + i, 0)),
        ),
        out_shape=jax.ShapeDtypeStruct((T_ALL, D_MODEL), BF16),
        compiler_params=_params("parallel", "parallel"),
        name="attn_prompt",
    )(sinks, q_all, kv_all, kv_all)


SAMPLE_KEYS = WIN_BUF + 16


SEQ_Q_ROWS = GQA_GROUP * DEC_SEQ
SEQ_S_ROWS = N_KV_HEADS * SEQ_Q_ROWS


def _attn_sample_kernel(*refs, precise):
    q_ref, kvn_ref, ck_ref, cv_ref, hp_ref = refs[:5]
    o_ref = refs[-1]
    t_q = lax.broadcasted_iota(jnp.int32, (SEQ_S_ROWS, SAMPLE_KEYS), 0) % DEC_SEQ
    s_k = lax.broadcasted_iota(jnp.int32, (SEQ_S_ROWS, SAMPLE_KEYS), 1)
    dist = WIN_BUF + t_q - s_k
    valid = (dist >= 0) & (dist < WINDOW) & (s_k < WIN_BUF + DEC_SEQ)
    dist_f = dist.astype(F32)
    slope, sink = hp_ref[:, 0:1], hp_ref[:, 1:2]
    col_head = lax.broadcasted_iota(jnp.int32, (SEQ_Q_ROWS, KV_DIM), 1) // HEAD_DIM
    pad = jnp.zeros((SAMPLE_KEYS - WIN_BUF - DEC_SEQ, KV_DIM), F32)
    q_all = q_ref[...].astype(F32)
    outs = []
    for b in range(SAMPLE_BATCH_BLOCK):
        q_tok = q_all[b * DEC_SEQ:(b + 1) * DEC_SEQ, :]
        q_b = jnp.concatenate([q_tok[:, g * KV_DIM:(g + 1) * KV_DIM] for g in range(GQA_GROUP)], axis=0)
        kvn = kvn_ref[b * DEC_SEQ:(b + 1) * DEC_SEQ, :]
        k_all = jnp.concatenate([ck_ref[b], kvn[:, :KV_DIM], pad], axis=0)
        v_all = jnp.concatenate([cv_ref[b], kvn[:, KV_DIM:], pad], axis=0)
        q_heads = jnp.concatenate([jnp.where(col_head == kvh, q_b, jnp.zeros_like(q_b))
                                   for kvh in range(N_KV_HEADS)], axis=0)
        s = _mm(q_heads, k_all, precise, contract_b=1)
        o = _softmax_pv(s, dist_f, valid, slope, sink, v_all, precise)
        out = jnp.zeros((SEQ_Q_ROWS, KV_DIM), F32)
        for kvh in range(N_KV_HEADS):
            out = jnp.where(col_head == kvh, o[kvh * SEQ_Q_ROWS:(kvh + 1) * SEQ_Q_ROWS, :], out)
        outs.append(jnp.concatenate([out[g * DEC_SEQ:(g + 1) * DEC_SEQ, :] for g in range(GQA_GROUP)], axis=1))
    o_ref[...] = jnp.concatenate(outs, axis=0).astype(o_ref.dtype)


def _attn_sample(head_params, q_all, kv_all, cache_k, cache_v, first_row, precise, dest=None):
    bb = SAMPLE_BATCH_BLOCK
    rows = bb * DEC_SEQ
    first = first_row // rows
    in_specs = [pl.BlockSpec((rows, D_MODEL), lambda i: (first + i, 0)),
                pl.BlockSpec((rows, 2 * KV_DIM), lambda i: (first + i, 0)),
                pl.BlockSpec((bb, WIN_BUF, KV_DIM), lambda i: (i, 0, 0)),
                pl.BlockSpec((bb, WIN_BUF, KV_DIM), lambda i: (i, 0, 0)),
                pl.BlockSpec((SEQ_S_ROWS, LANES), lambda i: (0, 0))]
    args = [q_all, kv_all, cache_k, cache_v, head_params]
    if dest is None:
        out_rows, out_first, dtype, aliases = T_SAMPLE, 0, F32 if precise else BF16, {}
    else:
        out_rows, out_first, dtype, aliases = dest.shape[0], first, dest.dtype, {len(args): 0}
        in_specs.append(pl.BlockSpec(memory_space=pl.ANY))
        args.append(dest)
    return pl.pallas_call(
        functools.partial(_attn_sample_kernel, precise=precise),
        grid=(DEC_BATCH // bb,),
        in_specs=in_specs,
        out_specs=pl.BlockSpec((rows, D_MODEL), lambda i: (out_first + i, 0)),
        out_shape=jax.ShapeDtypeStruct((out_rows, D_MODEL), dtype),
        input_output_aliases=aliases,
        compiler_params=_params("parallel"),
        name="attn_sample_precise" if precise else "attn_sample",
    )(*args)


ROUTER_COLS = LANES


def _route(logits):
    lane = lax.broadcasted_iota(jnp.int32, logits.shape, 1)
    is_group = (lane >= N_EXPERTS) & (lane < N_EXPERTS + N_GROUPS)
    gl = jnp.where(is_group, logits, -jnp.inf)
    g_max = jnp.max(gl, axis=-1, keepdims=True)
    g_idx = jnp.min(jnp.where(gl == g_max, lane - N_EXPERTS, N_GROUPS), axis=-1, keepdims=True)
    g_w = 1.0 / jnp.sum(jnp.exp(gl - g_max), axis=-1, keepdims=True)
    in_group = (lane < N_EXPERTS) & ((lane // EXPERTS_PER_GROUP) == g_idx)
    el = jnp.where(in_group, logits, -jnp.inf)
    m1 = jnp.max(el, axis=-1, keepdims=True)
    i1 = jnp.min(jnp.where(el == m1, lane, ROUTER_COLS), axis=-1, keepdims=True)
    el2 = jnp.where(lane == i1, -jnp.inf, el)
    m2 = jnp.max(el2, axis=-1, keepdims=True)
    i2 = jnp.min(jnp.where(el2 == m2, lane, ROUTER_COLS), axis=-1, keepdims=True)
    r = jnp.exp(m2 - m1)
    w1 = g_w / (1.0 + r)
    w2 = g_w * r / (1.0 + r)
    return i1, i2, w1, w2


MOE_TILE = TOKEN_TILE
ROW_ALIGN = 16
SLOTS = 2 * MOE_TILE + N_EXPERTS * ROW_ALIGN
X_COLS = D_MODEL + LANES
N_TILES = T_ALL // MOE_TILE
ITEM_ROWS = 1024
SUB_ROWS = 512
MAX_ITEMS = N_TILES * SLOTS // ITEM_ROWS + N_EXPERTS


def _split3(w):
    hi = w.astype(BF16)
    r = w - hi.astype(F32)
    mid = r.astype(BF16)
    lo = (r - mid.astype(F32)).astype(BF16)
    return hi.astype(F32), mid.astype(F32), lo.astype(F32)


def _lane_pack(lane, cols):
    out = jnp.zeros(lane.shape, F32)
    for k, c in enumerate(cols):
        out = jnp.where(lane == k, c, out)
    return out


def _post_kernel(*refs, patched):
    if patched:
        (a_ref, w_ref, b_ref, x_ref, g_ref, wr_ref, br_ref, px1_ref, plg_ref,
         x1_ref, xs_ref, meta_ref, cnt_ref) = refs
    else:
        a_ref, w_ref, b_ref, x_ref, g_ref, wr_ref, br_ref, x1_ref, xs_ref, meta_ref, cnt_ref = refs
    tile = pl.program_id(0)
    x1 = x_ref[...] + (_mm(a_ref[...], w_ref[...], False) + b_ref[...])
    if patched:
        row = lax.broadcasted_iota(jnp.int32, (MOE_TILE, 1), 0)
        tiles_per_seq = SEQ // MOE_TILE
        use = (tile >= T_PROMPT // MOE_TILE) | (
            (tile % tiles_per_seq == tiles_per_seq - 1) & (row >= MOE_TILE - BLOCK))
        x1 = jnp.where(use, px1_ref[...], x1)
    x1_ref[...] = x1
    h = _rms(x1, g_ref[...])
    logits = _mm(h, wr_ref[...], False) + br_ref[...]
    if patched:
        logits = jnp.where(use, plg_ref[...], logits)
    i1, i2, w1, w2 = _route(logits)

    lane = lax.broadcasted_iota(jnp.int32, (MOE_TILE, LANES), 1)
    a1, a2 = lane == i1, lane == i2
    assigned = jnp.where(a1 | a2, 1.0, 0.0).astype(BF16)
    r_t = lax.broadcasted_iota(jnp.int32, (MOE_TILE, MOE_TILE), 0)
    c_t = lax.broadcasted_iota(jnp.int32, (MOE_TILE, MOE_TILE), 1)
    before = jnp.where(c_t < r_t, 1.0, 0.0).astype(BF16)
    rank = jnp.dot(before, assigned, preferred_element_type=F32)
    count = rank[MOE_TILE - 1:, :] + assigned[MOE_TILE - 1:, :].astype(F32)
    padded = jnp.floor((count + (ROW_ALIGN - 1)) * (1.0 / ROW_ALIGN)) * ROW_ALIGN
    r_e = lax.broadcasted_iota(jnp.int32, (LANES, LANES), 0)
    c_e = lax.broadcasted_iota(jnp.int32, (LANES, LANES), 1)
    lower_experts = jnp.where(r_e < c_e, 1.0, 0.0).astype(BF16)
    start = jnp.dot(jnp.broadcast_to(padded, (8, LANES)).astype(BF16), lower_experts,
                    preferred_element_type=F32)[0:1, :]
    pos = start + rank
    s1 = jnp.sum(jnp.where(a1, pos, 0.0), axis=-1, keepdims=True)
    s2 = jnp.sum(jnp.where(a2, pos, 0.0), axis=-1, keepdims=True)
    meta = _lane_pack(lane, [s1, s2])
    meta_ref[...] = meta
    cnt_ref[0] = jnp.broadcast_to(padded, (8, LANES)).astype(jnp.int32)

    slots_t = meta.T
    s_iota = lax.broadcasted_iota(jnp.int32, (SLOTS, MOE_TILE), 0)
    p1 = jnp.where(s_iota == slots_t[0:1, :].astype(jnp.int32), 1.0, 0.0).astype(BF16)
    p2 = jnp.where(s_iota == slots_t[1:2, :].astype(jnp.int32), 1.0, 0.0).astype(BF16)
    xs_ref[:, :D_MODEL] = jnp.dot(p1 + p2, h.astype(BF16), preferred_element_type=F32).astype(BF16)
    wcols = (jnp.dot(p1, _lane_pack(lane, _split3(w1)).astype(BF16), preferred_element_type=F32)
             + jnp.dot(p2, _lane_pack(lane, _split3(w2)).astype(BF16), preferred_element_type=F32))
    xs_ref[:, D_MODEL:] = wcols.astype(BF16)


def _post(a, w, b, x, g, wr, br, patch=None):
    tile = lambda i: (i, 0)
    const = lambda i: (0, 0)
    in_specs = [pl.BlockSpec((MOE_TILE, D_MODEL), tile),
                pl.BlockSpec((D_MODEL, D_MODEL), const),
                pl.BlockSpec((1, D_MODEL), const),
                pl.BlockSpec((MOE_TILE, D_MODEL), tile),
                pl.BlockSpec((1, D_MODEL), const),
                pl.BlockSpec((D_MODEL, ROUTER_COLS), const),
                pl.BlockSpec((1, ROUTER_COLS), const)]
    args = [a, w, b, x, g, wr, br]
    if patch is not None:
        tiles_per_seq = SEQ // MOE_TILE
        pidx = lambda i: (jnp.where(i >= BATCH * tiles_per_seq, i - BATCH * (tiles_per_seq - 1),
                                    i // tiles_per_seq), 0)
        in_specs += [pl.BlockSpec((MOE_TILE, D_MODEL), pidx), pl.BlockSpec((MOE_TILE, ROUTER_COLS), pidx)]
        args += list(patch)
    return pl.pallas_call(
        functools.partial(_post_kernel, patched=patch is not None),
        grid=(N_TILES,),
        in_specs=in_specs,
        out_specs=[pl.BlockSpec((MOE_TILE, D_MODEL), tile),
                   pl.BlockSpec((SLOTS, X_COLS), tile),
                   pl.BlockSpec((MOE_TILE, LANES), tile),
                   pl.BlockSpec((1, 8, LANES), lambda i: (i, 0, 0))],
        out_shape=[jax.ShapeDtypeStruct(x.shape, F32),
                   jax.ShapeDtypeStruct((N_TILES * SLOTS, X_COLS), BF16),
                   jax.ShapeDtypeStruct((x.shape[0], LANES), F32),
                   jax.ShapeDtypeStruct((N_TILES, 8, LANES), jnp.int32)],
        compiler_params=_params("parallel"),
        name="mixer_out_route_sort",
    )(*args)


def _work_plan(counts):
    i32 = lambda v: v.astype(jnp.int32)
    tile_start = jnp.cumsum(counts, axis=1) - counts
    tile_rows = jnp.sum(counts, axis=1)
    seg_rows = jnp.sum(counts, axis=0)
    offset = (jnp.cumsum(counts, axis=0) - counts).T
    n_items_e = (seg_rows + ITEM_ROWS - 1) // ITEM_ROWS
    item_end = jnp.cumsum(n_items_e)
    m = jnp.arange(MAX_ITEMS, dtype=jnp.int32)
    expert = jnp.minimum(jnp.sum(m[:, None] >= item_end[None, :], axis=1), N_EXPERTS - 1)
    base = (m - (item_end - n_items_e)[expert]) * ITEM_ROWS
    off_m = offset[expert]
    lo = jnp.sum(off_m + counts.T[expert] <= base[:, None], axis=1)
    hi = jnp.sum(off_m < (base + ITEM_ROWS)[:, None], axis=1)
    rows = jnp.clip(seg_rows[expert] - base, 0, ITEM_ROWS)
    chunk_row = (jnp.arange(N_TILES, dtype=jnp.int32)[:, None] * SLOTS + tile_start).T
    return (i32(expert), i32(lo), i32(hi), i32(base), i32(rows), i32(item_end[-1]).reshape(1),
            i32(chunk_row.reshape(-1)), i32(counts.T.reshape(-1)), i32(offset.reshape(-1)),
            i32(jnp.arange(N_TILES) * SLOTS + tile_rows), i32(SLOTS - tile_rows))


def _expert_kernel(ie_ref, lo_ref, hi_ref, base_ref, rows_ref, n_ref, crow_ref, clen_ref, coff_ref,
                   tdst_ref, tlen_ref, xs_hbm, wg_ref, wu_ref, wd_ref, y_hbm,
                   xbuf, ybuf, wgu_scr, wd_scr, zbuf, gsem, ssem, zsem):
    k = pl.program_id(0)
    n = n_ref[0]

    def for_chunks(m, fn):
        e, base = ie_ref[m], base_ref[m]
        end = base + rows_ref[m]

        def body(i, carry):
            c = e * N_TILES + i
            first = jnp.maximum(coff_ref[c], base)
            rows = pl.multiple_of(jnp.minimum(coff_ref[c] + clen_ref[c], end) - first, ROW_ALIGN)

            @pl.when(rows > 0)
            def _():
                fn(pl.multiple_of(crow_ref[c] + first - coff_ref[c], ROW_ALIGN),
                   pl.multiple_of(first - base, ROW_ALIGN), rows)
            return carry

        lax.fori_loop(lo_ref[m], hi_ref[m], body, 0)

    def gather(m):
        slot = m % 2
        for_chunks(m, lambda src, dst, rows: pltpu.make_async_copy(
            xs_hbm.at[pl.ds(src, rows)], xbuf.at[slot, pl.ds(dst, rows)], gsem.at[slot]).start())

    def scatter(m):
        slot = m % 2
        for_chunks(m, lambda dst, src, rows: pltpu.make_async_copy(
            ybuf.at[slot, pl.ds(src, rows)], y_hbm.at[pl.ds(dst, rows)], ssem.at[slot]).start())

    def wait_item(buf, sem, m):
        slot = m % 2
        rows = pl.multiple_of(rows_ref[m], ROW_ALIGN)

        @pl.when(rows > 0)
        def _():
            pltpu.make_async_copy(buf.at[slot, pl.ds(0, rows)], buf.at[slot, pl.ds(0, rows)], sem.at[slot]).wait()

    def tails(fn):
        for t in range(N_TILES):
            rows = pl.multiple_of(tlen_ref[t], ROW_ALIGN)

            @pl.when(rows > 0)
            def _():
                fn(pltpu.make_async_copy(zbuf.at[pl.ds(0, rows)],
                                         y_hbm.at[pl.ds(pl.multiple_of(tdst_ref[t], ROW_ALIGN), rows)], zsem))

    @pl.when(k == 0)
    def _():
        xbuf[...] = jnp.zeros_like(xbuf)
        zbuf[...] = jnp.zeros_like(zbuf)
        tails(lambda cp: cp.start())

        @pl.when(n > 0)
        def _():
            gather(0)

    @pl.when(k < n)
    def _():
        @pl.when(k + 1 < n)
        def _():
            gather(k + 1)

        @pl.when((k == 0) | (ie_ref[k] != ie_ref[jnp.maximum(k - 1, 0)]))
        def _():
            wgu_scr[:, :D_EXPERT] = wg_ref[0, 0].astype(BF16)
            wgu_scr[:, D_EXPERT:] = wu_ref[0, 0].astype(BF16)
            wd_scr[...] = wd_ref[0, 0].astype(BF16)

        slot = k % 2
        wait_item(xbuf, gsem, k)

        @pl.when(k >= 2)
        def _():
            wait_item(ybuf, ssem, k - 2)

        for sb in range(ITEM_ROWS // SUB_ROWS):
            @pl.when(sb * SUB_ROWS < rows_ref[k])
            def _():
                r0 = sb * SUB_ROWS
                x = xbuf[slot, r0:r0 + SUB_ROWS, :]
                gu = jnp.dot(x[:, :D_MODEL], wgu_scr[...], preferred_element_type=F32)
                gate, up = gu[:, :D_EXPERT], gu[:, D_EXPERT:]
                hid = (gate * jax.nn.sigmoid(gate) * up).astype(BF16)
                y = jnp.dot(hid, wd_scr[...], preferred_element_type=F32)
                wparts = x[:, D_MODEL:].astype(F32)
                weight = wparts[:, 0:1] + wparts[:, 1:2] + wparts[:, 2:3]
                ybuf[slot, r0:r0 + SUB_ROWS, :] = (weight * y).astype(BF16)

        scatter(k)

        @pl.when(k == n - 1)
        def _():
            @pl.when(k >= 1)
            def _():
                wait_item(ybuf, ssem, k - 1)
            wait_item(ybuf, ssem, k)
            tails(lambda cp: cp.wait())


def _experts(plan, xs, wg, wu, wd, layer):
    wsel = lambda k, *p: (layer, p[0][jnp.minimum(k, jnp.maximum(p[5][0] - 1, 0))], 0, 0)
    any_space = pl.BlockSpec(memory_space=pl.ANY)
    return pl.pallas_call(
        _expert_kernel,
        grid_spec=pltpu.PrefetchScalarGridSpec(
            num_scalar_prefetch=len(plan),
            grid=(MAX_ITEMS,),
            in_specs=[any_space,
                      pl.BlockSpec((1, 1, D_MODEL, D_EXPERT), wsel),
                      pl.BlockSpec((1, 1, D_MODEL, D_EXPERT), wsel),
                      pl.BlockSpec((1, 1, D_EXPERT, D_MODEL), wsel)],
            out_specs=any_space,
            scratch_shapes=[pltpu.VMEM((2, ITEM_ROWS, X_COLS), BF16),
                            pltpu.VMEM((2, ITEM_ROWS, D_MODEL), BF16),
                            pltpu.VMEM((D_MODEL, 2 * D_EXPERT), BF16),
                            pltpu.VMEM((D_EXPERT, D_MODEL), BF16),
                            pltpu.VMEM((MOE_TILE, D_MODEL), BF16),
                            pltpu.SemaphoreType.DMA((2,)),
                            pltpu.SemaphoreType.DMA((2,)),
                            pltpu.SemaphoreType.DMA(())],
        ),
        out_shape=jax.ShapeDtypeStruct((N_TILES * SLOTS, D_MODEL), BF16),
        compiler_params=_params("arbitrary"),
        name="experts",
    )(*plan, xs, wg, wu, wd)


PROMPT_TILES = T_PROMPT // MOE_TILE


def _combine_kernel(y_ref, meta_ref, x1_ref, g_ref, *o_refs, final):
    meta = meta_ref[...]
    s_iota = lax.broadcasted_iota(jnp.int32, (MOE_TILE, SLOTS), 1)
    pick = ((s_iota == meta[:, 0:1].astype(jnp.int32)) | (s_iota == meta[:, 1:2].astype(jnp.int32)))
    moe = jnp.dot(jnp.where(pick, 1.0, 0.0).astype(BF16), y_ref[...], preferred_element_type=F32)
    x2 = x1_ref[...] + moe
    if not final:
        o_refs[0][...] = x2
        return
    y = _rms(x2, g_ref[...])
    is_prompt = pl.program_id(0) < PROMPT_TILES

    @pl.when(is_prompt)
    def _():
        o_refs[0][...] = y

    @pl.when(jnp.logical_not(is_prompt))
    def _():
        o_refs[1][...] = y


def _combine(y_local, meta, x1, g_final, final):
    tile = lambda i: (i, 0)
    if final:
        out_specs = [pl.BlockSpec((MOE_TILE, D_MODEL), lambda i: (jnp.minimum(i, PROMPT_TILES - 1), 0)),
                     pl.BlockSpec((MOE_TILE, D_MODEL), lambda i: (jnp.maximum(i - PROMPT_TILES, 0), 0))]
        out_shape = [jax.ShapeDtypeStruct((T_PROMPT, D_MODEL), F32), jax.ShapeDtypeStruct((T_SAMPLE, D_MODEL), F32)]
    else:
        out_specs, out_shape = pl.BlockSpec((MOE_TILE, D_MODEL), tile), jax.ShapeDtypeStruct(x1.shape, F32)
    return pl.pallas_call(
        functools.partial(_combine_kernel, final=final),
        grid=(N_TILES,),
        in_specs=[pl.BlockSpec((SLOTS, D_MODEL), tile),
                  pl.BlockSpec((MOE_TILE, LANES), tile),
                  pl.BlockSpec((MOE_TILE, D_MODEL), tile),
                  pl.BlockSpec((1, D_MODEL), lambda i: (0, 0))],
        out_specs=out_specs,
        out_shape=out_shape,
        compiler_params=_params("arbitrary" if final else "parallel"),
        name="moe_combine",
    )(y_local, meta, x1, g_final)


def _moe(xs, meta, counts, x1, wg, wu, wd, layer, g_final):
    plan = _work_plan(counts[:, 0, :N_EXPERTS])
    return _combine(_experts(plan, xs, wg, wu, wd, layer), meta, x1, g_final, final=layer == DEPTH - 1)


def _post_precise_kernel(src_ref, tok_ref, a_ref, w_ref, b_ref, x_ref, g_ref, wr_ref, br_ref, x1_ref, lg_ref):
    del tok_ref
    filled = src_ref[pl.program_id(0)] >= 0

    @pl.when(filled)
    def _():
        x1 = x_ref[...] + (_mm(a_ref[...], w_ref[...], True) + b_ref[...])
        x1_ref[...] = x1
        lg_ref[...] = _mm(_rms(x1, g_ref[...]), wr_ref[...], True) + br_ref[...]

    @pl.when(jnp.logical_not(filled))
    def _():
        x1_ref[...] = jnp.zeros_like(x1_ref)
        lg_ref[...] = jnp.zeros_like(lg_ref)


def _post_precise(src_blocks, tok_blocks, a_c, w, b, x, g, wr, br):
    n = src_blocks.shape[0]
    const = lambda i, s, t: (0, 0)
    slot = lambda i, s, t: (i, 0)
    return pl.pallas_call(
        _post_precise_kernel,
        grid_spec=pltpu.PrefetchScalarGridSpec(
            num_scalar_prefetch=2,
            grid=(n,),
            in_specs=[pl.BlockSpec((BLOCK, D_MODEL), lambda i, s, t: (jnp.maximum(s[i], 0), 0)),
                      pl.BlockSpec((D_MODEL, D_MODEL), const),
                      pl.BlockSpec((1, D_MODEL), const),
                      pl.BlockSpec((BLOCK, D_MODEL), lambda i, s, t: (t[i], 0)),
                      pl.BlockSpec((1, D_MODEL), const),
                      pl.BlockSpec((D_MODEL, ROUTER_COLS), const),
                      pl.BlockSpec((1, ROUTER_COLS), const)],
            out_specs=[pl.BlockSpec((BLOCK, D_MODEL), slot),
                       pl.BlockSpec((BLOCK, ROUTER_COLS), slot)],
        ),
        out_shape=[jax.ShapeDtypeStruct((n * BLOCK, D_MODEL), F32),
                   jax.ShapeDtypeStruct((n * BLOCK, ROUTER_COLS), F32)],
        compiler_params=_params("arbitrary"),
        name="mixer_out_route_precise",
    )(src_blocks, tok_blocks, a_c, w, b, x, g, wr, br)


def _glu(h, w_ref, b_ref):
    a = jnp.dot(h, w_ref[...], preferred_element_type=F32) + b_ref[...]
    return a[:, :D_MODEL] * jax.nn.sigmoid(a[:, D_MODEL:])


def _ln_swish(c, g, b):
    cc = c - jnp.mean(c, axis=-1, keepdims=True)
    var = jnp.mean(cc * cc, axis=-1, keepdims=True)
    y = cc * lax.rsqrt(var + EPS) * g + b
    return y * jax.nn.sigmoid(y)


def _conv_prompt_kernel(halo_ref, x_ref, g_ref, w1_ref, b1_ref, wdw_ref, bdw_ref, lng_ref, lnb_ref,
                        act_ref, tail_ref, u_scr, c_scr):
    j = pl.program_id(1)
    xh = jnp.concatenate([halo_ref[...], x_ref[...]], axis=0)
    u = _glu(_rms(xh, g_ref[...]).astype(BF16), w1_ref, b1_ref)
    n = CONV_HALO + CONV_TILE
    row = lax.broadcasted_iota(jnp.int32, (n, 1), 0)
    u_scr[0] = jnp.where((row < CONV_HALO) & (j == 0), 0.0, u)
    for r in range(1, SUBLANES):
        u_scr[r, 0:n - SUBLANES, :] = u_scr[0, r:n - SUBLANES + r, :]

    def rows(c, carry):
        base = pl.multiple_of(c * CONV_ROWS, CONV_ROWS)
        pieces = CONV_ROWS // SUBLANES

        def taps(group, count, acc):
            first = pl.multiple_of(base + group * TAP_UNROLL, SUBLANES)
            for tap in range(count):
                a, r = divmod(CONV_HALO - HIST + tap, SUBLANES)
                w = wdw_ref[group * TAP_UNROLL + tap]
                acc = tuple(acc[p] + w * u_scr[r, pl.ds(first + (a + p) * SUBLANES, SUBLANES), :]
                            for p in range(pieces))
            return acc

        acc = tuple(jnp.zeros((SUBLANES, D_MODEL), F32) + bdw_ref[...] for _ in range(pieces))
        groups = CONV_WIDTH // TAP_UNROLL
        acc = lax.fori_loop(0, groups, lambda g, acc: taps(g, TAP_UNROLL, acc), acc)
        acc = taps(groups, CONV_WIDTH - groups * TAP_UNROLL, acc)
        c_scr[pl.ds(base, CONV_ROWS), :] = jnp.concatenate(acc, axis=0)
        return carry

    lax.fori_loop(0, CONV_TILE // CONV_ROWS, rows, 0)
    act_ref[...] = _ln_swish(c_scr[...], lng_ref[...], lnb_ref[...]).astype(BF16)

    @pl.when(j == pl.num_programs(1) - 1)
    def _():
        tail_ref[0] = u_scr[0, CONV_TILE:CONV_TILE + CONV_HALO, :]


def _conv_prompt(x_all, g, w1, b1, wdw, bdw, lng, lnb):
    nt = SEQ // CONV_TILE
    per_halo = CONV_TILE // CONV_HALO
    const = lambda b, j: (0, 0)
    return pl.pallas_call(
        _conv_prompt_kernel,
        grid=(BATCH, nt),
        in_specs=[pl.BlockSpec((CONV_HALO, D_MODEL),
                               lambda b, j: (jnp.maximum((b * nt + j) * per_halo - 1, 0), 0)),
                  pl.BlockSpec((CONV_TILE, D_MODEL), lambda b, j: (b * nt + j, 0)),
                  pl.BlockSpec((1, D_MODEL), const),
                  pl.BlockSpec((D_MODEL, 2 * D_MODEL), const),
                  pl.BlockSpec((1, 2 * D_MODEL), const),
                  pl.BlockSpec((CONV_WIDTH, SUBLANES, D_MODEL), lambda b, j: (0, 0, 0)),
                  pl.BlockSpec((1, D_MODEL), const),
                  pl.BlockSpec((1, D_MODEL), const),
                  pl.BlockSpec((1, D_MODEL), const)],
        out_specs=[pl.BlockSpec((CONV_TILE, D_MODEL), lambda b, j: (b * nt + j, 0)),
                   pl.BlockSpec((1, CONV_HALO, D_MODEL), lambda b, j: (b, 0, 0))],
        out_shape=[jax.ShapeDtypeStruct((T_ALL, D_MODEL), BF16),
                   jax.ShapeDtypeStruct((BATCH, CONV_HALO, D_MODEL), F32)],
        scratch_shapes=[pltpu.VMEM((SUBLANES, CONV_HALO + CONV_TILE, D_MODEL), F32),
                        pltpu.VMEM((CONV_TILE, D_MODEL), F32)],
        compiler_params=_params("parallel", "arbitrary"),
        name="conv_prompt",
    )(x_all, x_all, g, w1, b1, wdw, bdw, lng, lnb)


SAMPLE_EXT = 40


def _conv_sample_kernel(x_ref, st_ref, g_ref, w1_ref, b1_ref, wsh_ref, bdw_ref, lng_ref, lnb_ref, dest_ref,
                        act_ref, u_ref):
    del dest_ref
    u = _glu(_rms(x_ref[...], g_ref[...]).astype(BF16), w1_ref, b1_ref)
    u_ref[...] = u
    pad = jnp.zeros((SAMPLE_EXT - HIST - DEC_SEQ, D_MODEL), F32)
    rows = []
    for b in range(SAMPLE_BATCH_BLOCK):
        ext = jnp.concatenate([st_ref[b], u[b * DEC_SEQ:(b + 1) * DEC_SEQ, :], pad], axis=0)
        for t in range(DEC_SEQ):
            rows.append(jnp.sum(wsh_ref[t] * ext, axis=0, keepdims=True))
    c = jnp.concatenate(rows, axis=0) + bdw_ref[...]
    act_ref[...] = _ln_swish(c, lng_ref[...], lnb_ref[...]).astype(BF16)


def _conv_sample(x_all, state, g, w1, b1, wsh, bdw, lng, lnb, dest):
    bb = SAMPLE_BATCH_BLOCK
    rows = bb * DEC_SEQ
    first = T_PROMPT // rows
    const = lambda i: (0, 0)
    return pl.pallas_call(
        _conv_sample_kernel,
        grid=(DEC_BATCH // bb,),
        in_specs=[pl.BlockSpec((rows, D_MODEL), lambda i: (first + i, 0)),
                  pl.BlockSpec((bb, HIST, D_MODEL), lambda i: (i, 0, 0)),
                  pl.BlockSpec((1, D_MODEL), const),
                  pl.BlockSpec((D_MODEL, 2 * D_MODEL), const),
                  pl.BlockSpec((1, 2 * D_MODEL), const),
                  pl.BlockSpec((DEC_SEQ, SAMPLE_EXT, D_MODEL), lambda i: (0, 0, 0)),
                  pl.BlockSpec((1, D_MODEL), const),
                  pl.BlockSpec((1, D_MODEL), const),
                  pl.BlockSpec((1, D_MODEL), const),
                  pl.BlockSpec(memory_space=pl.ANY)],
        out_specs=[pl.BlockSpec((rows, D_MODEL), lambda i: (first + i, 0)),
                   pl.BlockSpec((rows, D_MODEL), lambda i: (i, 0))],
        out_shape=[jax.ShapeDtypeStruct(dest.shape, dest.dtype),
                   jax.ShapeDtypeStruct((T_SAMPLE, D_MODEL), F32)],
        input_output_aliases={9: 0},
        compiler_params=_params("parallel"),
        name="conv_sample",
    )(x_all, state, g, w1, b1, wsh, bdw, lng, lnb, dest)


def _router_weights(w_gr, b_gr, w_er, b_er):
    w = jnp.zeros((D_MODEL, ROUTER_COLS), F32)
    w = w.at[:, :N_EXPERTS].set(w_er).at[:, N_EXPERTS:N_EXPERTS + N_GROUPS].set(w_gr)
    b = jnp.zeros((1, ROUTER_COLS), F32)
    b = b.at[0, :N_EXPERTS].set(b_er).at[0, N_EXPERTS:N_EXPERTS + N_GROUPS].set(b_gr)
    return w, b


def _shifted_taps(w_dw):
    frames = [jnp.pad(w_dw, ((t, SAMPLE_EXT - CONV_WIDTH - t), (0, 0))) for t in range(DEC_SEQ)]
    return jnp.stack(frames)


_NB = SEQ // BLOCK
_SAMPLE_BLOCKS = [T_PROMPT // BLOCK + i for i in range(T_SAMPLE // BLOCK)]
PRECISE_KV_BLOCKS = [b * _NB + i for b in range(BATCH) for i in (_NB - 2, _NB - 1)] + _SAMPLE_BLOCKS
PRECISE_Q_BLOCKS = [b * _NB + _NB - 1 for b in range(BATCH)] + _SAMPLE_BLOCKS
_PER_TILE = MOE_TILE // BLOCK
PATCH_SRC = [(b if s == _PER_TILE - 1 else -1) for b in range(BATCH) for s in range(_PER_TILE)] + list(
    range(BATCH, BATCH + len(_SAMPLE_BLOCKS)))
PATCH_TOK = [PRECISE_Q_BLOCKS[max(s, 0)] for s in PATCH_SRC]
assert _NB >= 2 and WIN_BUF == BLOCK and T_SAMPLE % MOE_TILE == 0 and SEQ % MOE_TILE == 0


def kernel(x_prompt, x_sample, cache_k, cache_v, state_conv, w_qkv, w_o, attn_sinks, w_pw1, b_pw1, w_dw, b_dw, conv_ln_g, conv_ln_b, w_pw2, b_pw2, norm_mix_g, norm_ffn_g, w_group_router, b_group_router, w_expert_router, b_expert_router, w_gate, w_up, w_down, final_norm_g):
    row = lambda v: v.reshape(1, -1)
    assert DEPTH >= 1
    outs = {}
    for layer in range(DEPTH):
        j = layer // 2
        g_ffn = row(norm_ffn_g[layer])
        wr, br = _router_weights(w_group_router[layer], b_group_router[layer],
                                 w_expert_router[layer], b_expert_router[layer])
        if layer % 2 == 0:
            g_mix = row(norm_mix_g[layer])
            ck = cache_k[j].reshape(DEC_BATCH, WIN_BUF, KV_DIM)
            cv = cache_v[j].reshape(DEC_BATCH, WIN_BUF, KV_DIM)
            perm = jnp.asarray(Q_PERM, jnp.int32)
            wqkv = jnp.concatenate([w_qkv[j][:, perm], w_qkv[j][:, D_MODEL:]], axis=1)
            wo = w_o[j][perm, :]
            row_head = jnp.asarray([kvh * GQA_GROUP + g for kvh in range(N_KV_HEADS)
                                    for g in range(GQA_GROUP) for _ in range(DEC_SEQ)], jnp.int32)
            slopes = jnp.asarray([_alibi_slope(h) for h in range(N_HEADS)], F32)
            head_params = jnp.zeros((SEQ_S_ROWS, LANES), F32)
            head_params = head_params.at[:, 0].set(slopes[row_head]).at[:, 1].set(attn_sinks[j][row_head])

            assert layer == 0
            q, kv, x = _qkv(x_prompt.reshape(T_PROMPT, D_MODEL), x_sample.reshape(T_SAMPLE, D_MODEL), g_mix,
                            wqkv.astype(BF16))
            mixed = _attn_prompt(attn_sinks[j], q, kv)
            mixed = _attn_sample(head_params, q, kv, ck, cv, T_PROMPT, False, dest=mixed)
            b_out = jnp.zeros((1, D_MODEL), F32)

            q_c, kv_c = _qkv_precise(jnp.asarray(PRECISE_KV_BLOCKS, jnp.int32), x, g_mix, wqkv)
            n_prompt_rows = 2 * BATCH * BLOCK
            o_cp = _attn_prompt_last_precise(attn_sinks[j], q_c, kv_c)
            o_cs = _attn_sample(head_params, q_c, kv_c, ck, cv, n_prompt_rows, True)
            patch = _post_precise(jnp.asarray(PATCH_SRC, jnp.int32), jnp.asarray(PATCH_TOK, jnp.int32),
                                  jnp.concatenate([o_cp, o_cs], axis=0), wo, b_out, x, g_ffn, wr, br)
            x1, xs, meta, counts = _post(mixed, wo.astype(BF16), b_out, x, g_ffn, wr.astype(BF16), br, patch)

            kv_p = kv_c[:n_prompt_rows].reshape(BATCH, 2, BLOCK, 2 * KV_DIM)[:, 1]
            kv_s = kv_c[n_prompt_rows:].reshape(DEC_BATCH, DEC_SEQ, 2 * KV_DIM)
            shape5 = lambda a: a.reshape(a.shape[0], WIN_BUF, N_KV_HEADS, HEAD_DIM)
            outs.setdefault("k_p", []).append(shape5(kv_p[..., :KV_DIM]))
            outs.setdefault("v_p", []).append(shape5(kv_p[..., KV_DIM:]))
            outs.setdefault("k_s", []).append(shape5(jnp.concatenate([ck[:, DEC_SEQ:], kv_s[..., :KV_DIM]], axis=1)))
            outs.setdefault("v_s", []).append(shape5(jnp.concatenate([cv[:, DEC_SEQ:], kv_s[..., KV_DIM:]], axis=1)))
        else:
            g = row(norm_mix_g[layer])
            w1, b1 = w_pw1[j].astype(BF16), row(b_pw1[j])
            bdw, lng, lnb = row(b_dw[j]), row(conv_ln_g[j]), row(conv_ln_b[j])
            taps = jnp.broadcast_to(w_dw[j][:, None, :], (CONV_WIDTH, SUBLANES, D_MODEL))
            mixed, tail = _conv_prompt(x, g, w1, b1, taps, bdw, lng, lnb)
            mixed, u_s = _conv_sample(x, state_conv[j], g, w1, b1, _shifted_taps(w_dw[j]), bdw, lng, lnb, mixed)
            x1, xs, meta, counts = _post(mixed, w_pw2[j].astype(BF16), row(b_pw2[j]), x, g_ffn,
                                         wr.astype(BF16), br)
            outs.setdefault("c_p", []).append(tail[:, CONV_HALO - HIST:])
            u_s = u_s.reshape(DEC_BATCH, DEC_SEQ, D_MODEL)
            outs.setdefault("c_s", []).append(jnp.concatenate([state_conv[j][:, DEC_SEQ:], u_s], axis=1))
        x = _moe(xs, meta, counts, x1, w_gate, w_up, w_down, layer, row(final_norm_g))
    y_prompt = x[0].reshape(BATCH, SEQ, D_MODEL)
    y_sample = x[1].reshape(DEC_BATCH, DEC_SEQ, D_MODEL)
    return (y_prompt, y_sample, jnp.stack(outs["k_p"]), jnp.stack(outs["v_p"]), jnp.stack(outs["c_p"]),
            jnp.stack(outs["k_s"]), jnp.stack(outs["v_s"]), jnp.stack(outs["c_s"]))
```

```python
import functools

import jax
import jax.numpy as jnp
from jax import lax
from jax.experimental import pallas as pl
from jax.experimental.pallas import tpu as pltpu

D_MODEL = 1024
BATCH = 4
SEQ = 4096
DEPTH = 2
DEC_BATCH = 128
DEC_SEQ = 4
HEAD_DIM = 64
N_HEADS = 16
N_KV_HEADS = 4
GQA_GROUP = 4
WINDOW = 128
BLOCK = 128
WIN_BUF = 128
CONV_WIDTH = 31
HIST = CONV_WIDTH - 1
N_GROUPS = 4
EXPERTS_PER_GROUP = 8
N_EXPERTS = 32
D_EXPERT = 256
EPS = 1e-6
NEG_INF = -1e30

T_PROMPT = BATCH * SEQ
T_SAMPLE = DEC_BATCH * DEC_SEQ
T_ALL = T_PROMPT + T_SAMPLE
KV_DIM = N_KV_HEADS * HEAD_DIM
QKV_OUT = D_MODEL + 2 * KV_DIM

LANES = 128
SUBLANES = 8
TOKEN_TILE = 512
CONV_TILE = 512
CONV_ROWS = 32
TAP_UNROLL = SUBLANES
CONV_HALO = 32
SAMPLE_BATCH_BLOCK = 8
VMEM_LIMIT = 48 * 1024 * 1024

F32 = jnp.float32
BF16 = jnp.bfloat16


def _params(*sem):
    return pltpu.CompilerParams(dimension_semantics=sem, vmem_limit_bytes=VMEM_LIMIT)


def _rms(x, g):
    return x * lax.rsqrt(jnp.mean(x * x, axis=-1, keepdims=True) + EPS) * g


def _mm(a, b, precise, contract_b=0):
    dims = (((1,), (contract_b,)), ((), ()))
    if precise:
        return lax.dot_general(a.astype(F32), b.astype(F32), dims, precision=lax.Precision.HIGHEST,
                               preferred_element_type=F32)
    return lax.dot_general(a.astype(BF16), b.astype(BF16), dims, preferred_element_type=F32)


def _qkv_kernel(x_ref, g_ref, w_ref, q_ref, kv_ref, *, precise):
    qkv = _mm(_rms(x_ref[...], g_ref[...]), w_ref[...], precise)
    q_ref[...] = (qkv[:, :D_MODEL] * (HEAD_DIM ** -0.5)).astype(q_ref.dtype)
    kv_ref[...] = qkv[:, D_MODEL:]


def _qkv_precise_kernel(tbl_ref, x_ref, g_ref, w_ref, q_ref, kv_ref):
    del tbl_ref
    _qkv_kernel(x_ref, g_ref, w_ref, q_ref, kv_ref, precise=True)


def _qkv_precise(blocks, x, g, w):
    n = blocks.shape[0]
    return pl.pallas_call(
        _qkv_precise_kernel,
        grid_spec=pltpu.PrefetchScalarGridSpec(
            num_scalar_prefetch=1,
            grid=(n,),
            in_specs=[pl.BlockSpec((BLOCK, D_MODEL), lambda i, t: (t[i], 0)),
                      pl.BlockSpec((1, D_MODEL), lambda i, t: (0, 0)),
                      pl.BlockSpec((D_MODEL, QKV_OUT), lambda i, t: (0, 0))],
            out_specs=[pl.BlockSpec((BLOCK, D_MODEL), lambda i, t: (i, 0)),
                       pl.BlockSpec((BLOCK, 2 * KV_DIM), lambda i, t: (i, 0))],
        ),
        out_shape=[jax.ShapeDtypeStruct((n * BLOCK, D_MODEL), F32),
                   jax.ShapeDtypeStruct((n * BLOCK, 2 * KV_DIM), F32)],
        compiler_params=_params("arbitrary"),
        name="qkv_proj_precise",
    )(blocks, x, g, w)


def _qkv_join_kernel(xp_ref, xs_ref, g_ref, w_ref, q_ref, kv_ref, x_ref):
    x_ref[...] = jnp.where(pl.program_id(0) < T_PROMPT // TOKEN_TILE, xp_ref[...], xs_ref[...])
    _qkv_kernel(x_ref, g_ref, w_ref, q_ref, kv_ref, precise=False)


def _qkv(x_prompt, x_sample, g, w):
    n_prompt, n_sample = T_PROMPT // TOKEN_TILE, T_SAMPLE // TOKEN_TILE
    tile = lambda i: (i, 0)
    return pl.pallas_call(
        _qkv_join_kernel,
        grid=(n_prompt + n_sample,),
        in_specs=[pl.BlockSpec((TOKEN_TILE, D_MODEL), lambda i: (jnp.minimum(i, n_prompt - 1), 0)),
                  pl.BlockSpec((TOKEN_TILE, D_MODEL), lambda i: (jnp.maximum(i - n_prompt, 0), 0)),
                  pl.BlockSpec((1, D_MODEL), lambda i: (0, 0)),
                  pl.BlockSpec((D_MODEL, QKV_OUT), lambda i: (0, 0))],
        out_specs=[pl.BlockSpec((TOKEN_TILE, D_MODEL), tile),
                   pl.BlockSpec((TOKEN_TILE, 2 * KV_DIM), tile),
                   pl.BlockSpec((TOKEN_TILE, D_MODEL), tile)],
        out_shape=[jax.ShapeDtypeStruct((T_ALL, D_MODEL), BF16),
                   jax.ShapeDtypeStruct((T_ALL, 2 * KV_DIM), F32),
                   jax.ShapeDtypeStruct((T_ALL, D_MODEL), F32)],
        compiler_params=_params("parallel"),
        name="qkv_proj",
    )(x_prompt, x_sample, g, w)


def _alibi_slope(head):
    return 2.0 ** (-8.0 * (head + 1) / N_HEADS)


def _q_col(kvh, g):
    return (g * N_KV_HEADS + kvh) * HEAD_DIM


Q_PERM = [(kvh * GQA_GROUP + g) * HEAD_DIM + d
          for g in range(GQA_GROUP) for kvh in range(N_KV_HEADS) for d in range(HEAD_DIM)]


def _softmax_pv(s, dist_f, valid, slope, sink, v, precise):
    logits = jnp.where(valid, s - slope * dist_f, NEG_INF)
    m = jnp.maximum(jnp.max(logits, axis=-1, keepdims=True), sink)
    p = jnp.exp(logits - m)
    denom = jnp.sum(p, axis=-1, keepdims=True) + jnp.exp(sink - m)
    return _mm(p, v, precise) / denom


def _attn_prompt_kernel(sink_ref, q_ref, kvp_ref, kvc_ref, o_ref, *, precise, first_block):
    q = q_ref[...]
    kv = jnp.concatenate([kvp_ref[...], kvc_ref[...]], axis=0)
    kv = kv if precise else kv.astype(BF16)
    qi = lax.broadcasted_iota(jnp.int32, (BLOCK, 2 * BLOCK), 0)
    kj = lax.broadcasted_iota(jnp.int32, (BLOCK, 2 * BLOCK), 1)
    dist = qi + BLOCK - kj
    valid = (dist >= 0) & (dist < WINDOW) & ((kj >= BLOCK) | jnp.logical_not(first_block()))
    dist_f = dist.astype(F32)
    for kvh in range(N_KV_HEADS):
        k_h = kv[:, kvh * HEAD_DIM:(kvh + 1) * HEAD_DIM]
        v_h = kv[:, KV_DIM + kvh * HEAD_DIM:KV_DIM + (kvh + 1) * HEAD_DIM]
        for g in range(GQA_GROUP):
            head = kvh * GQA_GROUP + g
            col = _q_col(kvh, g)
            s = _mm(q[:, col:col + HEAD_DIM], k_h, precise, contract_b=1)
            o = _softmax_pv(s, dist_f, valid, _alibi_slope(head), sink_ref[head], v_h, precise)
            o_ref[:, col:col + HEAD_DIM] = o.astype(o_ref.dtype)


def _attn_prompt_last_precise(sinks, q_c, kv_c):
    return pl.pallas_call(
        functools.partial(_attn_prompt_kernel, precise=True, first_block=lambda: SEQ // BLOCK == 1),
        grid_spec=pltpu.PrefetchScalarGridSpec(
            num_scalar_prefetch=1,
            grid=(BATCH,),
            in_specs=[pl.BlockSpec((BLOCK, D_MODEL), lambda b, s: (2 * b + 1, 0)),
                      pl.BlockSpec((BLOCK, 2 * KV_DIM), lambda b, s: (2 * b, 0)),
                      pl.BlockSpec((BLOCK, 2 * KV_DIM), lambda b, s: (2 * b + 1, 0))],
            out_specs=pl.BlockSpec((BLOCK, D_MODEL), lambda b, s: (b, 0)),
        ),
        out_shape=jax.ShapeDtypeStruct((BATCH * BLOCK, D_MODEL), F32),
        compiler_params=_params("arbitrary"),
        name="attn_prompt_precise",
    )(sinks, q_c, kv_c, kv_c)


def _attn_prompt(sinks, q_all, kv_all):
    nb = SEQ // BLOCK
    return pl.pallas_call(
        functools.partial(_attn_prompt_kernel, precise=False, first_block=lambda: pl.program_id(1) == 0),
        grid_spec=pltpu.PrefetchScalarGridSpec(
            num_scalar_prefetch=1,
            grid=(BATCH, nb),
            in_specs=[pl.BlockSpec((BLOCK, D_MODEL), lambda b, i, s: (b * nb + i, 0)),
                      pl.BlockSpec((BLOCK, 2 * KV_DIM), lambda b, i, s: (b * nb + jnp.maximum(i - 1, 0), 0)),
                      pl.BlockSpec((BLOCK, 2 * KV_DIM), lambda b, i, s: (b * nb + i, 0))],
            out_specs=pl.BlockSpec((BLOCK, D_MODEL), lambda b, i, s: (b * nb + i, 0)),
        ),
        out_shape=jax.ShapeDtypeStruct((T_ALL, D_MODEL), BF16),
        compiler_params=_params("parallel", "parallel"),
        name="attn_prompt",
    )(sinks, q_all, kv_all, kv_all)


SAMPLE_KEYS = WIN_BUF + 16


SEQ_Q_ROWS = GQA_GROUP * DEC_SEQ
SEQ_S_ROWS = N_KV_HEADS * SEQ_Q_ROWS


def _attn_sample_kernel(*refs, precise):
    q_ref, kvn_ref, ck_ref, cv_ref, hp_ref = refs[:5]
    if precise:
        o_ref, ks_ref, vs_ref = refs[5:]
    else:
        o_ref = refs[6]
    t_q = lax.broadcasted_iota(jnp.int32, (SEQ_S_ROWS, SAMPLE_KEYS), 0) % DEC_SEQ
    s_k = lax.broadcasted_iota(jnp.int32, (SEQ_S_ROWS, SAMPLE_KEYS), 1)
    dist = WIN_BUF + t_q - s_k
    valid = (dist >= 0) & (dist < WINDOW) & (s_k < WIN_BUF + DEC_SEQ)
    dist_f = dist.astype(F32)
    slope, sink = hp_ref[:, 0:1], hp_ref[:, 1:2]
    col_head = lax.broadcasted_iota(jnp.int32, (SEQ_Q_ROWS, KV_DIM), 1) // HEAD_DIM
    pad = jnp.zeros((SAMPLE_KEYS - WIN_BUF - DEC_SEQ, KV_DIM), F32)
    q_all = q_ref[...].astype(F32)
    outs = []
    for b in range(SAMPLE_BATCH_BLOCK):
        q_tok = q_all[b * DEC_SEQ:(b + 1) * DEC_SEQ, :]
        q_b = jnp.concatenate([q_tok[:, g * KV_DIM:(g + 1) * KV_DIM] for g in range(GQA_GROUP)], axis=0)
        kvn = kvn_ref[b * DEC_SEQ:(b + 1) * DEC_SEQ, :]
        k_all = jnp.concatenate([ck_ref[b], kvn[:, :KV_DIM], pad], axis=0)
        v_all = jnp.concatenate([cv_ref[b], kvn[:, KV_DIM:], pad], axis=0)
        if precise:
            ks_ref[b] = k_all[DEC_SEQ:DEC_SEQ + WIN_BUF, :]
            vs_ref[b] = v_all[DEC_SEQ:DEC_SEQ + WIN_BUF, :]
        q_heads = jnp.concatenate([jnp.where(col_head == kvh, q_b, jnp.zeros_like(q_b))
                                   for kvh in range(N_KV_HEADS)], axis=0)
        s = _mm(q_heads, k_all, precise, contract_b=1)
        o = _softmax_pv(s, dist_f, valid, slope, sink, v_all, precise)
        out = jnp.zeros((SEQ_Q_ROWS, KV_DIM), F32)
        for kvh in range(N_KV_HEADS):
            out = jnp.where(col_head == kvh, o[kvh * SEQ_Q_ROWS:(kvh + 1) * SEQ_Q_ROWS, :], out)
        outs.append(jnp.concatenate([out[g * DEC_SEQ:(g + 1) * DEC_SEQ, :] for g in range(GQA_GROUP)], axis=1))
    o_ref[...] = jnp.concatenate(outs, axis=0).astype(o_ref.dtype)


def _attn_sample(head_params, q_all, kv_all, cache_k, cache_v, first_row, dest=None):
    precise = dest is None
    bb = SAMPLE_BATCH_BLOCK
    rows = bb * DEC_SEQ
    first = first_row // rows
    in_specs = [pl.BlockSpec((rows, D_MODEL), lambda i: (first + i, 0)),
                pl.BlockSpec((rows, 2 * KV_DIM), lambda i: (first + i, 0)),
                pl.BlockSpec((bb, WIN_BUF, KV_DIM), lambda i: (i, 0, 0)),
                pl.BlockSpec((bb, WIN_BUF, KV_DIM), lambda i: (i, 0, 0)),
                pl.BlockSpec((SEQ_S_ROWS, LANES), lambda i: (0, 0))]
    args = [q_all, kv_all, cache_k, cache_v, head_params]
    if precise:
        window = pl.BlockSpec((bb, WIN_BUF, KV_DIM), lambda i: (i, 0, 0))
        out_specs = [pl.BlockSpec((rows, D_MODEL), lambda i: (i, 0)), window, window]
        out_shape = [jax.ShapeDtypeStruct((T_SAMPLE, D_MODEL), F32),
                     jax.ShapeDtypeStruct((DEC_BATCH, WIN_BUF, KV_DIM), F32),
                     jax.ShapeDtypeStruct((DEC_BATCH, WIN_BUF, KV_DIM), F32)]
        aliases = {}
    else:
        out_specs = pl.BlockSpec((rows, D_MODEL), lambda i: (first + i, 0))
        out_shape = jax.ShapeDtypeStruct(dest.shape, dest.dtype)
        aliases = {len(args): 0}
        in_specs.append(pl.BlockSpec(memory_space=pl.ANY))
        args.append(dest)
    return pl.pallas_call(
        functools.partial(_attn_sample_kernel, precise=precise),
        grid=(DEC_BATCH // bb,),
        in_specs=in_specs,
        out_specs=out_specs,
        out_shape=out_shape,
        input_output_aliases=aliases,
        compiler_params=_params("parallel"),
        name="attn_sample_precise" if precise else "attn_sample",
    )(*args)


ROUTER_COLS = LANES


def _route(logits):
    lane = lax.broadcasted_iota(jnp.int32, logits.shape, 1)
    is_group = (lane >= N_EXPERTS) & (lane < N_EXPERTS + N_GROUPS)
    gl = jnp.where(is_group, logits, -jnp.inf)
    g_max = jnp.max(gl, axis=-1, keepdims=True)
    g_idx = jnp.min(jnp.where(gl == g_max, lane - N_EXPERTS, N_GROUPS), axis=-1, keepdims=True)
    g_w = 1.0 / jnp.sum(jnp.exp(gl - g_max), axis=-1, keepdims=True)
    in_group = (lane < N_EXPERTS) & ((lane // EXPERTS_PER_GROUP) == g_idx)
    el = jnp.where(in_group, logits, -jnp.inf)
    m1 = jnp.max(el, axis=-1, keepdims=True)
    i1 = jnp.min(jnp.where(el == m1, lane, ROUTER_COLS), axis=-1, keepdims=True)
    el2 = jnp.where(lane == i1, -jnp.inf, el)
    m2 = jnp.max(el2, axis=-1, keepdims=True)
    i2 = jnp.min(jnp.where(el2 == m2, lane, ROUTER_COLS), axis=-1, keepdims=True)
    r = jnp.exp(m2 - m1)
    w1 = g_w / (1.0 + r)
    w2 = g_w * r / (1.0 + r)
    return i1, i2, w1, w2


MOE_TILE = TOKEN_TILE
ROW_ALIGN = 16
SLOTS = 2 * MOE_TILE + N_EXPERTS * ROW_ALIGN
X_COLS = D_MODEL + LANES
WEIGHT_KEY_LANE = 6
N_TILES = T_ALL // MOE_TILE
ITEM_ROWS = 1024
SUB_ROWS = 512
MAX_ITEMS = N_TILES * SLOTS // ITEM_ROWS + N_EXPERTS


def _split3(w):
    hi = w.astype(BF16)
    r = w - hi.astype(F32)
    mid = r.astype(BF16)
    lo = (r - mid.astype(F32)).astype(BF16)
    return hi.astype(F32), mid.astype(F32), lo.astype(F32)


def _lane_pack(lane, cols):
    out = jnp.zeros(lane.shape, F32)
    for k, c in enumerate(cols):
        out = jnp.where(lane == k, c, out)
    return out


def _post_kernel(*refs, patched):
    if patched:
        (a_ref, w_ref, b_ref, x_ref, g_ref, wr_ref, br_ref, px1_ref, plg_ref,
         x1_ref, xs_ref, meta_ref, cnt_ref) = refs
    else:
        a_ref, w_ref, b_ref, x_ref, g_ref, wr_ref, br_ref, x1_ref, xs_ref, meta_ref, cnt_ref = refs
    tile = pl.program_id(0)
    x1 = x_ref[...] + (_mm(a_ref[...], w_ref[...], False) + b_ref[...])
    if patched:
        row = lax.broadcasted_iota(jnp.int32, (MOE_TILE, 1), 0)
        tiles_per_seq = SEQ // MOE_TILE
        use = (tile >= T_PROMPT // MOE_TILE) | (
            (tile % tiles_per_seq == tiles_per_seq - 1) & (row >= MOE_TILE - BLOCK))
        x1 = jnp.where(use, px1_ref[...], x1)
    x1_ref[...] = x1
    h = _rms(x1, g_ref[...])
    logits = _mm(h, wr_ref[...], False) + br_ref[...]
    if patched:
        logits = jnp.where(use, plg_ref[...], logits)
    i1, i2, w1, w2 = _route(logits)

    lane = lax.broadcasted_iota(jnp.int32, (MOE_TILE, LANES), 1)
    a1, a2 = lane == i1, lane == i2
    assigned = jnp.where(a1 | a2, 1.0, 0.0).astype(BF16)
    r_t = lax.broadcasted_iota(jnp.int32, (MOE_TILE, MOE_TILE), 0)
    c_t = lax.broadcasted_iota(jnp.int32, (MOE_TILE, MOE_TILE), 1)
    before = jnp.where(c_t < r_t, 1.0, 0.0).astype(BF16)
    rank = jnp.dot(before, assigned, preferred_element_type=F32)
    count = rank[MOE_TILE - 1:, :] + assigned[MOE_TILE - 1:, :].astype(F32)
    padded = jnp.floor((count + (ROW_ALIGN - 1)) * (1.0 / ROW_ALIGN)) * ROW_ALIGN
    r_e = lax.broadcasted_iota(jnp.int32, (LANES, LANES), 0)
    c_e = lax.broadcasted_iota(jnp.int32, (LANES, LANES), 1)
    lower_experts = jnp.where(r_e < c_e, 1.0, 0.0).astype(BF16)
    start = jnp.dot(jnp.broadcast_to(padded, (8, LANES)).astype(BF16), lower_experts,
                    preferred_element_type=F32)[0:1, :]
    pos = start + rank
    s1 = jnp.sum(jnp.where(a1, pos, 0.0), axis=-1, keepdims=True)
    s2 = jnp.sum(jnp.where(a2, pos, 0.0), axis=-1, keepdims=True)
    meta = _lane_pack(lane, [s1, s2])
    meta_ref[...] = meta
    cnt_ref[0] = jnp.broadcast_to(padded, (8, LANES)).astype(jnp.int32)

    slots_t = meta.T
    s_iota = lax.broadcasted_iota(jnp.int32, (SLOTS, MOE_TILE), 0)
    pick = ((s_iota == slots_t[0:1, :].astype(jnp.int32)) | (s_iota == slots_t[1:2, :].astype(jnp.int32)))
    perm = jnp.where(pick, 1.0, 0.0).astype(BF16)
    xs_ref[:, :D_MODEL] = jnp.dot(perm, h.astype(BF16), preferred_element_type=F32).astype(BF16)
    carried = _lane_pack(lane, [*_split3(w1), *_split3(w2), i1.astype(F32)]).astype(BF16)
    xs_ref[:, D_MODEL:] = jnp.dot(perm, carried, preferred_element_type=F32).astype(BF16)


def _post(a, w, b, x, g, wr, br, patch=None):
    tile = lambda i: (i, 0)
    const = lambda i: (0, 0)
    in_specs = [pl.BlockSpec((MOE_TILE, D_MODEL), tile),
                pl.BlockSpec((D_MODEL, D_MODEL), const),
                pl.BlockSpec((1, D_MODEL), const),
                pl.BlockSpec((MOE_TILE, D_MODEL), tile),
                pl.BlockSpec((1, D_MODEL), const),
                pl.BlockSpec((D_MODEL, ROUTER_COLS), const),
                pl.BlockSpec((1, ROUTER_COLS), const)]
    args = [a, w, b, x, g, wr, br]
    if patch is not None:
        tiles_per_seq = SEQ // MOE_TILE
        pidx = lambda i: (jnp.where(i >= BATCH * tiles_per_seq, i - BATCH * (tiles_per_seq - 1),
                                    i // tiles_per_seq), 0)
        in_specs += [pl.BlockSpec((MOE_TILE, D_MODEL), pidx), pl.BlockSpec((MOE_TILE, ROUTER_COLS), pidx)]
        args += list(patch)
    return pl.pallas_call(
        functools.partial(_post_kernel, patched=patch is not None),
        grid=(N_TILES,),
        in_specs=in_specs,
        out_specs=[pl.BlockSpec((MOE_TILE, D_MODEL), tile),
                   pl.BlockSpec((SLOTS, X_COLS), tile),
                   pl.BlockSpec((MOE_TILE, LANES), tile),
                   pl.BlockSpec((1, 8, LANES), lambda i: (i, 0, 0))],
        out_shape=[jax.ShapeDtypeStruct(x.shape, F32),
                   jax.ShapeDtypeStruct((N_TILES * SLOTS, X_COLS), BF16),
                   jax.ShapeDtypeStruct((x.shape[0], LANES), F32),
                   jax.ShapeDtypeStruct((N_TILES, 8, LANES), jnp.int32)],
        compiler_params=_params("parallel"),
        name="mixer_out_route_sort",
    )(*args)


def _work_plan(counts):
    i32 = lambda v: v.astype(jnp.int32)
    tile_start = jnp.cumsum(counts, axis=1) - counts
    tile_rows = jnp.sum(counts, axis=1)
    seg_rows = jnp.sum(counts, axis=0)
    offset = (jnp.cumsum(counts, axis=0) - counts).T
    n_items_e = (seg_rows + ITEM_ROWS - 1) // ITEM_ROWS
    item_end = jnp.cumsum(n_items_e)
    m = jnp.arange(MAX_ITEMS, dtype=jnp.int32)
    expert = jnp.minimum(jnp.sum(m[:, None] >= item_end[None, :], axis=1), N_EXPERTS - 1)
    base = (m - (item_end - n_items_e)[expert]) * ITEM_ROWS
    off_m = offset[expert]
    lo = jnp.sum(off_m + counts.T[expert] <= base[:, None], axis=1)
    hi = jnp.sum(off_m < (base + ITEM_ROWS)[:, None], axis=1)
    rows = jnp.clip(seg_rows[expert] - base, 0, ITEM_ROWS)
    chunk_row = (jnp.arange(N_TILES, dtype=jnp.int32)[:, None] * SLOTS + tile_start).T
    return (i32(expert), i32(lo), i32(hi), i32(base), i32(rows), i32(item_end[-1]).reshape(1),
            i32(chunk_row.reshape(-1)), i32(counts.T.reshape(-1)), i32(offset.reshape(-1)),
            i32(jnp.arange(N_TILES) * SLOTS + tile_rows), i32(SLOTS - tile_rows))


def _expert_kernel(ie_ref, lo_ref, hi_ref, base_ref, rows_ref, n_ref, crow_ref, clen_ref, coff_ref,
                   tdst_ref, tlen_ref, xs_hbm, wg_ref, wu_ref, wd_ref, y_hbm,
                   xbuf, ybuf, wgu_scr, wd_scr, zbuf, gsem, ssem, zsem):
    k = pl.program_id(0)
    n = n_ref[0]

    def for_chunks(m, fn):
        e, base = ie_ref[m], base_ref[m]
        end = base + rows_ref[m]

        def body(i, carry):
            c = e * N_TILES + i
            first = jnp.maximum(coff_ref[c], base)
            rows = pl.multiple_of(jnp.minimum(coff_ref[c] + clen_ref[c], end) - first, ROW_ALIGN)

            @pl.when(rows > 0)
            def _():
                fn(pl.multiple_of(crow_ref[c] + first - coff_ref[c], ROW_ALIGN),
                   pl.multiple_of(first - base, ROW_ALIGN), rows)
            return carry

        lax.fori_loop(lo_ref[m], hi_ref[m], body, 0)

    def gather(m):
        slot = m % 2
        for_chunks(m, lambda src, dst, rows: pltpu.make_async_copy(
            xs_hbm.at[pl.ds(src, rows)], xbuf.at[slot, pl.ds(dst, rows)], gsem.at[slot]).start())

    def scatter(m):
        slot = m % 2
        for_chunks(m, lambda dst, src, rows: pltpu.make_async_copy(
            ybuf.at[slot, pl.ds(src, rows)], y_hbm.at[pl.ds(dst, rows)], ssem.at[slot]).start())

    def wait_item(buf, sem, m):
        slot = m % 2
        rows = pl.multiple_of(rows_ref[m], ROW_ALIGN)

        @pl.when(rows > 0)
        def _():
            pltpu.make_async_copy(buf.at[slot, pl.ds(0, rows)], buf.at[slot, pl.ds(0, rows)], sem.at[slot]).wait()

    def tails(fn):
        for t in range(N_TILES):
            rows = pl.multiple_of(tlen_ref[t], ROW_ALIGN)

            @pl.when(rows > 0)
            def _():
                fn(pltpu.make_async_copy(zbuf.at[pl.ds(0, rows)],
                                         y_hbm.at[pl.ds(pl.multiple_of(tdst_ref[t], ROW_ALIGN), rows)], zsem))

    @pl.when(k == 0)
    def _():
        xbuf[...] = jnp.zeros_like(xbuf)
        zbuf[...] = jnp.zeros_like(zbuf)
        tails(lambda cp: cp.start())

        @pl.when(n > 0)
        def _():
            gather(0)

    @pl.when(k < n)
    def _():
        @pl.when(k + 1 < n)
        def _():
            gather(k + 1)

        @pl.when((k == 0) | (ie_ref[k] != ie_ref[jnp.maximum(k - 1, 0)]))
        def _():
            wgu_scr[:, :D_EXPERT] = wg_ref[0, 0].astype(BF16)
            wgu_scr[:, D_EXPERT:] = wu_ref[0, 0].astype(BF16)
            wd_scr[...] = wd_ref[0, 0].astype(BF16)

        slot = k % 2
        wait_item(xbuf, gsem, k)

        @pl.when(k >= 2)
        def _():
            wait_item(ybuf, ssem, k - 2)

        for sb in range(ITEM_ROWS // SUB_ROWS):
            @pl.when(sb * SUB_ROWS < rows_ref[k])
            def _():
                r0 = sb * SUB_ROWS
                x = xbuf[slot, r0:r0 + SUB_ROWS, :]
                gu = jnp.dot(x[:, :D_MODEL], wgu_scr[...], preferred_element_type=F32)
                gate, up = gu[:, :D_EXPERT], gu[:, D_EXPERT:]
                hid = (gate * jax.nn.sigmoid(gate) * up).astype(BF16)
                y = jnp.dot(hid, wd_scr[...], preferred_element_type=F32)
                wparts = x[:, D_MODEL:].astype(F32)
                is_first = wparts[:, WEIGHT_KEY_LANE:WEIGHT_KEY_LANE + 1] == ie_ref[k].astype(F32)
                weight = jnp.where(is_first, wparts[:, 0:1] + wparts[:, 1:2] + wparts[:, 2:3],
                                   wparts[:, 3:4] + wparts[:, 4:5] + wparts[:, 5:6])
                ybuf[slot, r0:r0 + SUB_ROWS, :] = (weight * y).astype(BF16)

        scatter(k)

        @pl.when(k == n - 1)
        def _():
            @pl.when(k >= 1)
            def _():
                wait_item(ybuf, ssem, k - 1)
            wait_item(ybuf, ssem, k)
            tails(lambda cp: cp.wait())


def _experts(plan, xs, wg, wu, wd, layer):
    wsel = lambda k, *p: (layer, p[0][jnp.minimum(k, jnp.maximum(p[5][0] - 1, 0))], 0, 0)
    any_space = pl.BlockSpec(memory_space=pl.ANY)
    return pl.pallas_call(
        _expert_kernel,
        grid_spec=pltpu.PrefetchScalarGridSpec(
            num_scalar_prefetch=len(plan),
            grid=(MAX_ITEMS,),
            in_specs=[any_space,
                      pl.BlockSpec((1, 1, D_MODEL, D_EXPERT), wsel),
                      pl.BlockSpec((1, 1, D_MODEL, D_EXPERT), wsel),
                      pl.BlockSpec((1, 1, D_EXPERT, D_MODEL), wsel)],
            out_specs=any_space,
            scratch_shapes=[pltpu.VMEM((2, ITEM_ROWS, X_COLS), BF16),
                            pltpu.VMEM((2, ITEM_ROWS, D_MODEL), BF16),
                            pltpu.VMEM((D_MODEL, 2 * D_EXPERT), BF16),
                            pltpu.VMEM((D_EXPERT, D_MODEL), BF16),
                            pltpu.VMEM((MOE_TILE, D_MODEL), BF16),
                            pltpu.SemaphoreType.DMA((2,)),
                            pltpu.SemaphoreType.DMA((2,)),
                            pltpu.SemaphoreType.DMA(())],
        ),
        out_shape=jax.ShapeDtypeStruct((N_TILES * SLOTS, D_MODEL), BF16),
        compiler_params=_params("arbitrary"),
        name="experts",
    )(*plan, xs, wg, wu, wd)


PROMPT_TILES = T_PROMPT // MOE_TILE


def _combine_kernel(y_ref, meta_ref, x1_ref, g_ref, *o_refs, final):
    meta = meta_ref[...]
    s_iota = lax.broadcasted_iota(jnp.int32, (MOE_TILE, SLOTS), 1)
    pick = ((s_iota == meta[:, 0:1].astype(jnp.int32)) | (s_iota == meta[:, 1:2].astype(jnp.int32)))
    moe = jnp.dot(jnp.where(pick, 1.0, 0.0).astype(BF16), y_ref[...], preferred_element_type=F32)
    x2 = x1_ref[...] + moe
    if not final:
        o_refs[0][...] = x2
        return
    y = _rms(x2, g_ref[...])
    is_prompt = pl.program_id(0) < PROMPT_TILES

    @pl.when(is_prompt)
    def _():
        o_refs[0][...] = y

    @pl.when(jnp.logical_not(is_prompt))
    def _():
        o_refs[1][...] = y


def _combine(y_local, meta, x1, g_final, final):
    tile = lambda i: (i, 0)
    if final:
        out_specs = [pl.BlockSpec((MOE_TILE, D_MODEL), lambda i: (jnp.minimum(i, PROMPT_TILES - 1), 0)),
                     pl.BlockSpec((MOE_TILE, D_MODEL), lambda i: (jnp.maximum(i - PROMPT_TILES, 0), 0))]
        out_shape = [jax.ShapeDtypeStruct((T_PROMPT, D_MODEL), F32), jax.ShapeDtypeStruct((T_SAMPLE, D_MODEL), F32)]
    else:
        out_specs, out_shape = pl.BlockSpec((MOE_TILE, D_MODEL), tile), jax.ShapeDtypeStruct(x1.shape, F32)
    return pl.pallas_call(
        functools.partial(_combine_kernel, final=final),
        grid=(N_TILES,),
        in_specs=[pl.BlockSpec((SLOTS, D_MODEL), tile),
                  pl.BlockSpec((MOE_TILE, LANES), tile),
                  pl.BlockSpec((MOE_TILE, D_MODEL), tile),
                  pl.BlockSpec((1, D_MODEL), lambda i: (0, 0))],
        out_specs=out_specs,
        out_shape=out_shape,
        compiler_params=_params("arbitrary" if final else "parallel"),
        name="moe_combine",
    )(y_local, meta, x1, g_final)


def _moe(xs, meta, counts, x1, wg, wu, wd, layer, g_final):
    plan = _work_plan(counts[:, 0, :N_EXPERTS])
    return _combine(_experts(plan, xs, wg, wu, wd, layer), meta, x1, g_final, final=layer == DEPTH - 1)


def _post_precise_kernel(src_ref, tok_ref, a_ref, w_ref, b_ref, x_ref, g_ref, wr_ref, br_ref, x1_ref, lg_ref):
    del tok_ref
    filled = src_ref[pl.program_id(0)] >= 0

    @pl.when(filled)
    def _():
        x1 = x_ref[...] + (_mm(a_ref[...], w_ref[...], True) + b_ref[...])
        x1_ref[...] = x1
        lg_ref[...] = _mm(_rms(x1, g_ref[...]), wr_ref[...], True) + br_ref[...]

    @pl.when(jnp.logical_not(filled))
    def _():
        x1_ref[...] = jnp.zeros_like(x1_ref)
        lg_ref[...] = jnp.zeros_like(lg_ref)


def _post_precise(src_blocks, tok_blocks, a_c, w, b, x, g, wr, br):
    n = src_blocks.shape[0]
    const = lambda i, s, t: (0, 0)
    slot = lambda i, s, t: (i, 0)
    return pl.pallas_call(
        _post_precise_kernel,
        grid_spec=pltpu.PrefetchScalarGridSpec(
            num_scalar_prefetch=2,
            grid=(n,),
            in_specs=[pl.BlockSpec((BLOCK, D_MODEL), lambda i, s, t: (jnp.maximum(s[i], 0), 0)),
                      pl.BlockSpec((D_MODEL, D_MODEL), const),
                      pl.BlockSpec((1, D_MODEL), const),
                      pl.BlockSpec((BLOCK, D_MODEL), lambda i, s, t: (t[i], 0)),
                      pl.BlockSpec((1, D_MODEL), const),
                      pl.BlockSpec((D_MODEL, ROUTER_COLS), const),
                      pl.BlockSpec((1, ROUTER_COLS), const)],
            out_specs=[pl.BlockSpec((BLOCK, D_MODEL), slot),
                       pl.BlockSpec((BLOCK, ROUTER_COLS), slot)],
        ),
        out_shape=[jax.ShapeDtypeStruct((n * BLOCK, D_MODEL), F32),
                   jax.ShapeDtypeStruct((n * BLOCK, ROUTER_COLS), F32)],
        compiler_params=_params("arbitrary"),
        name="mixer_out_route_precise",
    )(src_blocks, tok_blocks, a_c, w, b, x, g, wr, br)


def _glu(h, w_ref, b_ref):
    a = jnp.dot(h, w_ref[...], preferred_element_type=F32) + b_ref[...]
    return a[:, :D_MODEL] * jax.nn.sigmoid(a[:, D_MODEL:])


def _ln_swish(c, g, b):
    cc = c - jnp.mean(c, axis=-1, keepdims=True)
    var = jnp.mean(cc * cc, axis=-1, keepdims=True)
    y = cc * lax.rsqrt(var + EPS) * g + b
    return y * jax.nn.sigmoid(y)


def _conv_prompt_kernel(halo_ref, x_ref, g_ref, w1_ref, b1_ref, wdw_ref, bdw_ref, lng_ref, lnb_ref,
                        act_ref, tail_ref, u_scr, c_scr):
    j = pl.program_id(1)
    xh = jnp.concatenate([halo_ref[...], x_ref[...]], axis=0)
    u = _glu(_rms(xh, g_ref[...]).astype(BF16), w1_ref, b1_ref)
    n = CONV_HALO + CONV_TILE
    row = lax.broadcasted_iota(jnp.int32, (n, 1), 0)
    u_scr[0] = jnp.where((row < CONV_HALO) & (j == 0), 0.0, u)
    for r in range(1, SUBLANES):
        u_scr[r, 0:n - SUBLANES, :] = u_scr[0, r:n - SUBLANES + r, :]

    def rows(c, carry):
        base = pl.multiple_of(c * CONV_ROWS, CONV_ROWS)
        pieces = CONV_ROWS // SUBLANES

        def taps(group, count, acc):
            first = pl.multiple_of(base + group * TAP_UNROLL, SUBLANES)
            for tap in range(count):
                a, r = divmod(CONV_HALO - HIST + tap, SUBLANES)
                w = wdw_ref[group * TAP_UNROLL + tap]
                acc = tuple(acc[p] + w * u_scr[r, pl.ds(first + (a + p) * SUBLANES, SUBLANES), :]
                            for p in range(pieces))
            return acc

        acc = tuple(jnp.zeros((SUBLANES, D_MODEL), F32) + bdw_ref[...] for _ in range(pieces))
        groups = CONV_WIDTH // TAP_UNROLL
        acc = lax.fori_loop(0, groups, lambda g, acc: taps(g, TAP_UNROLL, acc), acc)
        acc = taps(groups, CONV_WIDTH - groups * TAP_UNROLL, acc)
        c_scr[pl.ds(base, CONV_ROWS), :] = jnp.concatenate(acc, axis=0)
        return carry

    lax.fori_loop(0, CONV_TILE // CONV_ROWS, rows, 0)
    act_ref[...] = _ln_swish(c_scr[...], lng_ref[...], lnb_ref[...]).astype(BF16)

    @pl.when(j == pl.num_programs(1) - 1)
    def _():
        tail_ref[0] = u_scr[0, CONV_TILE:CONV_TILE + CONV_HALO, :]


def _conv_prompt(x_all, g, w1, b1, wdw, bdw, lng, lnb):
    nt = SEQ // CONV_TILE
    per_halo = CONV_TILE // CONV_HALO
    const = lambda b, j: (0, 0)
    return pl.pallas_call(
        _conv_prompt_kernel,
        grid=(BATCH, nt),
        in_specs=[pl.BlockSpec((CONV_HALO, D_MODEL),
                               lambda b, j: (jnp.maximum((b * nt + j) * per_halo - 1, 0), 0)),
                  pl.BlockSpec((CONV_TILE, D_MODEL), lambda b, j: (b * nt + j, 0)),
                  pl.BlockSpec((1, D_MODEL), const),
                  pl.BlockSpec((D_MODEL, 2 * D_MODEL), const),
                  pl.BlockSpec((1, 2 * D_MODEL), const),
                  pl.BlockSpec((CONV_WIDTH, SUBLANES, D_MODEL), lambda b, j: (0, 0, 0)),
                  pl.BlockSpec((1, D_MODEL), const),
                  pl.BlockSpec((1, D_MODEL), const),
                  pl.BlockSpec((1, D_MODEL), const)],
        out_specs=[pl.BlockSpec((CONV_TILE, D_MODEL), lambda b, j: (b * nt + j, 0)),
                   pl.BlockSpec((1, CONV_HALO, D_MODEL), lambda b, j: (b, 0, 0))],
        out_shape=[jax.ShapeDtypeStruct((T_ALL, D_MODEL), BF16),
                   jax.ShapeDtypeStruct((BATCH, CONV_HALO, D_MODEL), F32)],
        scratch_shapes=[pltpu.VMEM((SUBLANES, CONV_HALO + CONV_TILE, D_MODEL), F32),
                        pltpu.VMEM((CONV_TILE, D_MODEL), F32)],
        compiler_params=_params("parallel", "arbitrary"),
        name="conv_prompt",
    )(x_all, x_all, g, w1, b1, wdw, bdw, lng, lnb)


SAMPLE_EXT = 40


def _conv_sample_kernel(x_ref, st_ref, g_ref, w1_ref, b1_ref, wsh_ref, bdw_ref, lng_ref, lnb_ref, dest_ref,
                        act_ref, st_out_ref):
    del dest_ref
    u = _glu(_rms(x_ref[...], g_ref[...]).astype(BF16), w1_ref, b1_ref)
    pad = jnp.zeros((SAMPLE_EXT - HIST - DEC_SEQ, D_MODEL), F32)
    rows = []
    for b in range(SAMPLE_BATCH_BLOCK):
        ext = jnp.concatenate([st_ref[b], u[b * DEC_SEQ:(b + 1) * DEC_SEQ, :], pad], axis=0)
        st_out_ref[b] = ext[DEC_SEQ:DEC_SEQ + HIST, :]
        for t in range(DEC_SEQ):
            rows.append(jnp.sum(wsh_ref[t] * ext, axis=0, keepdims=True))
    c = jnp.concatenate(rows, axis=0) + bdw_ref[...]
    act_ref[...] = _ln_swish(c, lng_ref[...], lnb_ref[...]).astype(BF16)


def _conv_sample(x_all, state, g, w1, b1, wsh, bdw, lng, lnb, dest):
    bb = SAMPLE_BATCH_BLOCK
    rows = bb * DEC_SEQ
    first = T_PROMPT // rows
    const = lambda i: (0, 0)
    return pl.pallas_call(
        _conv_sample_kernel,
        grid=(DEC_BATCH // bb,),
        in_specs=[pl.BlockSpec((rows, D_MODEL), lambda i: (first + i, 0)),
                  pl.BlockSpec((bb, HIST, D_MODEL), lambda i: (i, 0, 0)),
                  pl.BlockSpec((1, D_MODEL), const),
                  pl.BlockSpec((D_MODEL, 2 * D_MODEL), const),
                  pl.BlockSpec((1, 2 * D_MODEL), const),
                  pl.BlockSpec((DEC_SEQ, SAMPLE_EXT, D_MODEL), lambda i: (0, 0, 0)),
                  pl.BlockSpec((1, D_MODEL), const),
                  pl.BlockSpec((1, D_MODEL), const),
                  pl.BlockSpec((1, D_MODEL), const),
                  pl.BlockSpec(memory_space=pl.ANY)],
        out_specs=[pl.BlockSpec((rows, D_MODEL), lambda i: (first + i, 0)),
                   pl.BlockSpec((bb, HIST, D_MODEL), lambda i: (i, 0, 0))],
        out_shape=[jax.ShapeDtypeStruct(dest.shape, dest.dtype),
                   jax.ShapeDtypeStruct((DEC_BATCH, HIST, D_MODEL), F32)],
        input_output_aliases={9: 0},
        compiler_params=_params("parallel"),
        name="conv_sample",
    )(x_all, state, g, w1, b1, wsh, bdw, lng, lnb, dest)


def _router_weights(w_gr, b_gr, w_er, b_er):
    w = jnp.zeros((D_MODEL, ROUTER_COLS), F32)
    w = w.at[:, :N_EXPERTS].set(w_er).at[:, N_EXPERTS:N_EXPERTS + N_GROUPS].set(w_gr)
    b = jnp.zeros((1, ROUTER_COLS), F32)
    b = b.at[0, :N_EXPERTS].set(b_er).at[0, N_EXPERTS:N_EXPERTS + N_GROUPS].set(b_gr)
    return w, b


def _shifted_taps(w_dw):
    frames = [jnp.pad(w_dw, ((t, SAMPLE_EXT - CONV_WIDTH - t), (0, 0))) for t in range(DEC_SEQ)]
    return jnp.stack(frames)


_NB = SEQ // BLOCK
_SAMPLE_BLOCKS = [T_PROMPT // BLOCK + i for i in range(T_SAMPLE // BLOCK)]
PRECISE_KV_BLOCKS = [b * _NB + i for b in range(BATCH) for i in (_NB - 2, _NB - 1)] + _SAMPLE_BLOCKS
PRECISE_Q_BLOCKS = [b * _NB + _NB - 1 for b in range(BATCH)] + _SAMPLE_BLOCKS
_PER_TILE = MOE_TILE // BLOCK
PATCH_SRC = [(b if s == _PER_TILE - 1 else -1) for b in range(BATCH) for s in range(_PER_TILE)] + list(
    range(BATCH, BATCH + len(_SAMPLE_BLOCKS)))
PATCH_TOK = [PRECISE_Q_BLOCKS[max(s, 0)] for s in PATCH_SRC]
assert _NB >= 2 and WIN_BUF == BLOCK and T_SAMPLE % MOE_TILE == 0 and SEQ % MOE_TILE == 0


def kernel(x_prompt, x_sample, cache_k, cache_v, state_conv, w_qkv, w_o, attn_sinks, w_pw1, b_pw1, w_dw, b_dw, conv_ln_g, conv_ln_b, w_pw2, b_pw2, norm_mix_g, norm_ffn_g, w_group_router, b_group_router, w_expert_router, b_expert_router, w_gate, w_up, w_down, final_norm_g):
    row = lambda v: v.reshape(1, -1)
    assert DEPTH >= 1
    outs = {}
    for layer in range(DEPTH):
        j = layer // 2
        g_ffn = row(norm_ffn_g[layer])
        wr, br = _router_weights(w_group_router[layer], b_group_router[layer],
                                 w_expert_router[layer], b_expert_router[layer])
        if layer % 2 == 0:
            g_mix = row(norm_mix_g[layer])
            ck = cache_k[j].reshape(DEC_BATCH, WIN_BUF, KV_DIM)
            cv = cache_v[j].reshape(DEC_BATCH, WIN_BUF, KV_DIM)
            perm = jnp.asarray(Q_PERM, jnp.int32)
            wqkv = jnp.concatenate([w_qkv[j][:, perm], w_qkv[j][:, D_MODEL:]], axis=1)
            wo = w_o[j][perm, :]
            row_head = jnp.asarray([kvh * GQA_GROUP + g for kvh in range(N_KV_HEADS)
                                    for g in range(GQA_GROUP) for _ in range(DEC_SEQ)], jnp.int32)
            slopes = jnp.asarray([_alibi_slope(h) for h in range(N_HEADS)], F32)
            head_params = jnp.zeros((SEQ_S_ROWS, LANES), F32)
            head_params = head_params.at[:, 0].set(slopes[row_head]).at[:, 1].set(attn_sinks[j][row_head])

            assert layer == 0
            q, kv, x = _qkv(x_prompt.reshape(T_PROMPT, D_MODEL), x_sample.reshape(T_SAMPLE, D_MODEL), g_mix,
                            wqkv.astype(BF16))
            mixed = _attn_prompt(attn_sinks[j], q, kv)
            mixed = _attn_sample(head_params, q, kv, ck, cv, T_PROMPT, dest=mixed)
            b_out = jnp.zeros((1, D_MODEL), F32)

            q_c, kv_c = _qkv_precise(jnp.asarray(PRECISE_KV_BLOCKS, jnp.int32), x, g_mix, wqkv)
            n_prompt_rows = 2 * BATCH * BLOCK
            o_cp = _attn_prompt_last_precise(attn_sinks[j], q_c, kv_c)
            o_cs, k_s, v_s = _attn_sample(head_params, q_c, kv_c, ck, cv, n_prompt_rows)
            patch = _post_precise(jnp.asarray(PATCH_SRC, jnp.int32), jnp.asarray(PATCH_TOK, jnp.int32),
                                  jnp.concatenate([o_cp, o_cs], axis=0), wo, b_out, x, g_ffn, wr, br)
            x1, xs, meta, counts = _post(mixed, wo.astype(BF16), b_out, x, g_ffn, wr.astype(BF16), br, patch)

            kv_p = kv_c[:n_prompt_rows].reshape(BATCH, 2, BLOCK, 2 * KV_DIM)[:, 1]
            shape5 = lambda a: a.reshape(a.shape[0], WIN_BUF, N_KV_HEADS, HEAD_DIM)
            outs.setdefault("k_p", []).append(shape5(kv_p[..., :KV_DIM]))
            outs.setdefault("v_p", []).append(shape5(kv_p[..., KV_DIM:]))
            outs.setdefault("k_s", []).append(shape5(k_s))
            outs.setdefault("v_s", []).append(shape5(v_s))
        else:
            g = row(norm_mix_g[layer])
            w1, b1 = w_pw1[j].astype(BF16), row(b_pw1[j])
            bdw, lng, lnb = row(b_dw[j]), row(conv_ln_g[j]), row(conv_ln_b[j])
            taps = jnp.broadcast_to(w_dw[j][:, None, :], (CONV_WIDTH, SUBLANES, D_MODEL))
            mixed, tail = _conv_prompt(x, g, w1, b1, taps, bdw, lng, lnb)
            mixed, c_s = _conv_sample(x, state_conv[j], g, w1, b1, _shifted_taps(w_dw[j]), bdw, lng, lnb, mixed)
            x1, xs, meta, counts = _post(mixed, w_pw2[j].astype(BF16), row(b_pw2[j]), x, g_ffn,
                                         wr.astype(BF16), br)
            outs.setdefault("c_p", []).append(tail[:, CONV_HALO - HIST:])
            outs.setdefault("c_s", []).append(c_s)
        x = _moe(xs, meta, counts, x1, w_gate, w_up, w_down, layer, row(final_norm_g))
    y_prompt = x[0].reshape(BATCH, SEQ, D_MODEL)
    y_sample = x[1].reshape(DEC_BATCH, DEC_SEQ, D_MODEL)
    return (y_prompt, y_sample, jnp.stack(outs["k_p"]), jnp.stack(outs["v_p"]), jnp.stack(outs["c_p"]),
            jnp.stack(outs["k_s"]), jnp.stack(outs["v_s"]), jnp.stack(outs["c_s"]))
```
